```python
import jax, jax.numpy as jnp
from jax import lax
import numpy as np

D_MODEL = 2048
BATCH = 4
SEQ = 8192
DEPTH = 2

HEAD_DIM = 128
N_HEADS = D_MODEL // HEAD_DIM
N_HEADS_DIL = N_HEADS // 2
N_HEADS_NA = N_HEADS - N_HEADS_DIL
ROT_DIM = HEAD_DIM // 4
ROPE_THETA = 500000.0
DIL_PATTERNS = ((128, 1), (512, 4), (2048, 16))
BAND_BLOCK = 64
GRID_W = 64
NA_ROWS_MAX = 8
NA_COLS = 16
NA_Q_COLS = 16
NA_K_COLS = 32
D_FF = 4 * D_MODEL
EPS = 1e-6
NEG_INF = -1e30
W_DIL = 3 * N_HEADS_DIL * HEAD_DIM
W_NA = 3 * N_HEADS_NA * HEAD_DIM

kernel_name = 'hybrid_dilated_neighborhood_encoder'


def rms_norm_f32(x, g):
    xf = x.astype(jnp.float32)
    y = xf * lax.rsqrt(jnp.mean(xf * xf, axis=-1, keepdims=True) + EPS)
    return y * g.astype(jnp.float32)


def partial_rope(x):
    S = x.shape[1]
    pos = jnp.arange(S, dtype=jnp.float32)
    inv_freq = ROPE_THETA ** (-jnp.arange(0, ROT_DIM, 2, dtype=jnp.float32) / ROT_DIM)
    ang = pos[:, None] * inv_freq[None, :]
    cos = jnp.cos(ang)[None, :, None, :]
    sin = jnp.sin(ang)[None, :, None, :]
    half = ROT_DIM // 2
    x1 = x[..., :half]
    x2 = x[..., half:ROT_DIM]
    return jnp.concatenate([x1 * cos - x2 * sin, x2 * cos + x1 * sin, x[..., ROT_DIM:]], axis=-1)


def dilated_window_attn(q, k, v, window, dilation):
    B, S, H, hd = q.shape
    half = window // (2 * dilation)
    L = S // dilation
    nb = -(-L // BAND_BLOCK)
    Lp = nb * BAND_BLOCK
    pad = Lp - L

    def to_res(t):
        return t.reshape(B, L, dilation, H, hd).transpose(0, 2, 3, 1, 4)

    qr = jnp.pad(to_res(q), ((0, 0), (0, 0), (0, 0), (0, pad), (0, 0)))
    qb = qr.reshape(B, dilation, H, nb, BAND_BLOCK, hd)

    def band(t):
        tp = jnp.pad(to_res(t), ((0, 0), (0, 0), (0, 0), (BAND_BLOCK, BAND_BLOCK + pad), (0, 0)))
        tp = tp.reshape(B, dilation, H, nb + 2, BAND_BLOCK, hd)
        return jnp.concatenate([tp[:, :, :, :-2], tp[:, :, :, 1:-1], tp[:, :, :, 2:]], axis=4)

    kn = band(k)
    vn = band(v)
    n_i = np.arange(nb)[:, None, None]
    q_i = np.arange(BAND_BLOCK)[None, :, None]
    k_i = np.arange(3 * BAND_BLOCK)[None, None, :]
    rel = k_i - BAND_BLOCK - q_i
    key_abs = (n_i - 1) * BAND_BLOCK + k_i
    valid = (np.abs(rel) <= half) & (key_abs >= 0) & (key_abs < L)

    s = jnp.einsum('bzhnqd,bzhnkd->bzhnqk', qb, kn)
    s = jnp.where(valid, s, NEG_INF)
    m = jnp.max(s, axis=-1)
    p = jnp.exp(s - m[..., None])
    l = jnp.sum(p, axis=-1)
    o = jnp.einsum('bzhnqk,bzhnkd->bzhnqd', p, vn) / l[..., None]

    o = o.reshape(B, dilation, H, Lp, hd)[:, :, :, :L].transpose(0, 3, 1, 2, 4).reshape(B, S, H, hd)
    m = m.reshape(B, dilation, H, Lp)[..., :L].transpose(0, 3, 1, 2).reshape(B, S, H)
    l = l.reshape(B, dilation, H, Lp)[..., :L].transpose(0, 3, 1, 2).reshape(B, S, H)
    return o, m, l


def dilated_mixer(q, k, v):
    outs = [dilated_window_attn(q, k, v, w, d) for (w, d) in DIL_PATTERNS]
    m_all = jnp.stack([o[1] for o in outs], axis=0)
    m_max = jnp.max(m_all, axis=0)
    wts = jnp.stack([o[2] for o in outs], axis=0) * jnp.exp(m_all - m_max[None])
    o_all = jnp.stack([o[0] for o in outs], axis=0)
    return jnp.sum(wts[..., None] * o_all, axis=0) / jnp.sum(wts, axis=0)[..., None]


def neighborhood_attn(q, k, v, rpb):
    B, S, H, hd = q.shape
    rows = S // GRID_W
    kr = min(NA_ROWS_MAX, rows)
    ncb = GRID_W // NA_Q_COLS

    kstart = np.clip(np.arange(ncb) * NA_Q_COLS - NA_COLS // 2, 0, GRID_W - NA_K_COLS)
    colidx = kstart[:, None] + np.arange(NA_K_COLS)[None, :]
    qcol = (np.arange(ncb)[:, None] * NA_Q_COLS + np.arange(NA_Q_COLS)[None, :])[:, :, None]
    keycol = colidx[:, None, :]
    cs = np.clip(qcol - NA_COLS // 2, 0, GRID_W - NA_COLS)
    colmask = (keycol >= cs) & (keycol < cs + NA_COLS)
    coff = np.clip(keycol - qcol + NA_COLS - 1, 0, 2 * NA_COLS - 2)
    rpb_c = rpb.astype(jnp.float32)[:, :, coff]

    kg = k.reshape(B, rows, GRID_W, H, hd).transpose(0, 3, 1, 2, 4)
    vg = v.reshape(B, rows, GRID_W, H, hd).transpose(0, 3, 1, 2, 4)
    qg = q.reshape(B, rows, GRID_W, H, hd).transpose(1, 0, 3, 2, 4)

    def row_step(args):
        i, q_row = args
        rs = jnp.clip(i - kr // 2, 0, rows - kr)
        k_blk = lax.dynamic_slice_in_dim(kg, rs, kr, axis=2)[:, :, :, colidx]
        v_blk = lax.dynamic_slice_in_dim(vg, rs, kr, axis=2)[:, :, :, colidx]
        qb = q_row.reshape(B, H, ncb, NA_Q_COLS, hd)
        s = jnp.einsum('bhcqd,bhrckd->bhcqrk', qb, k_blk)
        roff = rs + jnp.arange(kr) - i + NA_ROWS_MAX - 1
        bias = rpb_c[:, roff].transpose(0, 2, 3, 1, 4)
        s = jnp.where(colmask[:, :, None, :], s + bias[None], NEG_INF)
        p = jax.nn.softmax(s.reshape(B, H, ncb, NA_Q_COLS, kr * NA_K_COLS), axis=-1)
        p = p.reshape(B, H, ncb, NA_Q_COLS, kr, NA_K_COLS)
        o = jnp.einsum('bhcqrk,bhrckd->bhcqd', p, v_blk)
        return o.reshape(B, H, GRID_W, hd)

    out = lax.map(row_step, (jnp.arange(rows), qg))
    return out.transpose(1, 0, 3, 2, 4).reshape(B, S, H, hd)


def setup_inputs(seed: int = 0) -> dict:
    key = jax.random.key(seed)
    ks = jax.random.split(key, 16)
    f32 = jnp.float32
    d = D_MODEL
    nrel_r = 2 * NA_ROWS_MAX - 1
    nrel_c = 2 * NA_COLS - 1
    return {
        'x': jax.random.normal(ks[0], (BATCH, SEQ, d), f32),
        'c': jax.random.normal(ks[1], (BATCH, d), f32),
        'ln1': 1.0 + 0.05 * jax.random.normal(ks[2], (DEPTH, d), f32),
        'w_ada': 0.5 * d ** -0.5 * jax.random.normal(ks[3], (DEPTH, d, 6 * d), f32),
        'b_ada': 0.01 * jax.random.normal(ks[4], (DEPTH, 6 * d), f32),
        'w_in': d ** -0.5 * jax.random.normal(ks[5], (DEPTH, d, W_DIL + W_NA), f32),
        'q_norm_dil': 1.0 + 0.05 * jax.random.normal(ks[6], (DEPTH, HEAD_DIM), f32),
        'k_norm_dil': 1.0 + 0.05 * jax.random.normal(ks[7], (DEPTH, HEAD_DIM), f32),
        'q_norm_na': 1.0 + 0.05 * jax.random.normal(ks[8], (DEPTH, HEAD_DIM), f32),
        'k_norm_na': 1.0 + 0.05 * jax.random.normal(ks[9], (DEPTH, HEAD_DIM), f32),
        'na_rel_bias': 0.1 * jax.random.normal(ks[10], (DEPTH, N_HEADS_NA, nrel_r, nrel_c), f32),
        'w_out': d ** -0.5 * jax.random.normal(ks[11], (DEPTH, d, d), f32),
        'ln2': 1.0 + 0.05 * jax.random.normal(ks[12], (DEPTH, d), f32),
        'w_mlp_in': d ** -0.5 * jax.random.normal(ks[13], (DEPTH, d, D_FF), f32),
        'w_mlp_out': D_FF ** -0.5 * jax.random.normal(ks[14], (DEPTH, D_FF, d), f32),
    }


def reference(x, c, ln1, w_ada, b_ada, w_in, q_norm_dil, k_norm_dil, q_norm_na, k_norm_na,
              na_rel_bias, w_out, ln2, w_mlp_in, w_mlp_out):
    B, S, D = x.shape
    scale = HEAD_DIM ** -0.5
    c_act = jax.nn.silu(c)
    for layer in range(DEPTH):
        mod = (c_act @ w_ada[layer] + b_ada[layer]).astype(jnp.float32)
        sh1, sc1, g1, sh2, sc2, g2 = jnp.split(mod[:, None, :], 6, axis=-1)

        h = (rms_norm_f32(x, ln1[layer]) * (1.0 + sc1) + sh1).astype(x.dtype)
        proj = h @ w_in[layer]
        p_dil = proj[..., :W_DIL].astype(jnp.float32).reshape(B, S, 3, N_HEADS_DIL, HEAD_DIM)
        p_na = proj[..., W_DIL:].astype(jnp.float32).reshape(B, S, 3, N_HEADS_NA, HEAD_DIM)

        qa = partial_rope(rms_norm_f32(p_dil[:, :, 0], q_norm_dil[layer])) * scale
        ka = partial_rope(rms_norm_f32(p_dil[:, :, 1], k_norm_dil[layer]))
        out_a = dilated_mixer(qa, ka, p_dil[:, :, 2])

        qb = rms_norm_f32(p_na[:, :, 0], q_norm_na[layer]) * scale
        kb = rms_norm_f32(p_na[:, :, 1], k_norm_na[layer])
        out_b = neighborhood_attn(qb, kb, p_na[:, :, 2], na_rel_bias[layer])

        mixed = jnp.concatenate([out_a, out_b], axis=2).reshape(B, S, D).astype(x.dtype)
        x = (x + g1 * (mixed @ w_out[layer])).astype(x.dtype)

        h2 = (rms_norm_f32(x, ln2[layer]) * (1.0 + sc2) + sh2).astype(x.dtype)
        hid = jnp.square(jax.nn.relu(h2 @ w_mlp_in[layer]))
        x = (x + g2 * (hid @ w_mlp_out[layer])).astype(x.dtype)
    return x
```

```python
import functools

import numpy as np
import jax
import jax.numpy as jnp
from jax import lax
from jax.experimental import pallas as pl
from jax.experimental.pallas import tpu as pltpu

HEAD_DIM = 128
N_HEADS_DIL = 8
N_HEADS_NA = 8
ROT_DIM = HEAD_DIM // 4
ROPE_THETA = 500000.0
DIL_HALF = 64
DILATIONS = (16, 4, 1)
GRID_W = 64
NA_ROWS = 8
NA_COLS = 16
EPS = 1e-6
NEG_INF = -1e30

V7X_VMEM_BYTES = 64 * 1024 * 1024
VMEM_LIMIT = 56 * 1024 * 1024

F32 = jnp.float32
BF16 = jnp.bfloat16


def _cparams(sem):
    return pltpu.CompilerParams(dimension_semantics=sem, vmem_limit_bytes=VMEM_LIMIT)


def _mod_kernel(ct_ref, w_ref, b_ref, o_ref, act_s, *, nb):
    d = w_ref.shape[0]
    tn = w_ref.shape[1]
    ct = ct_ref[...]
    act_s[...] = ct * (1.0 / (1.0 + jnp.exp(-ct)))

    def body(k, accs):
        r0 = pl.multiple_of(k * 8, 8)
        w8 = w_ref[pl.ds(r0, 8), :]
        a8 = act_s[pl.ds(r0, 8), :]
        return tuple(acc + w8 * a8[:, b:b + 1] for b, acc in enumerate(accs))

    accs = lax.fori_loop(0, d // 8, body, tuple(jnp.zeros((8, tn), F32) for _ in range(nb)))
    for b in range(nb):
        o_ref[b:b + 1, :] = jnp.sum(accs[b], axis=0, keepdims=True) + b_ref[...]


def _modulation(c, w_ada, b_ada):
    nl, d, n6 = w_ada.shape
    nb = c.shape[0]
    tn = 1024
    return pl.pallas_call(
        functools.partial(_mod_kernel, nb=nb),
        grid=(nl, n6 // tn),
        in_specs=[
            pl.BlockSpec((d, nb), lambda l, j: (0, 0)),
            pl.BlockSpec((None, d, tn), lambda l, j: (l, 0, j)),
            pl.BlockSpec((None, 1, tn), lambda l, j: (l, 0, j)),
        ],
        out_specs=pl.BlockSpec((None, nb, tn), lambda l, j: (l, 0, j)),
        out_shape=jax.ShapeDtypeStruct((nl, nb, n6), F32),
        scratch_shapes=[pltpu.VMEM((d, nb), F32)],
        compiler_params=_cparams(("arbitrary", "arbitrary")),
        name="adaln_modulation",
    )(c.T, w_ada, b_ada.reshape(nl, 1, n6))


_NORM_CHUNK = 64


def _norm_modulate(x_ref, ln_ref, sc, sh, h_s):
    tm = x_ref.shape[0]
    gain = ln_ref[...] * (1.0 + sc)

    def body(c, carry):
        r0 = pl.multiple_of(c * _NORM_CHUNK, _NORM_CHUNK)
        x = x_ref[pl.ds(r0, _NORM_CHUNK), :]
        ms = jnp.mean(x * x, axis=-1, keepdims=True)
        h_s[pl.ds(r0, _NORM_CHUNK), :] = (x * lax.rsqrt(ms + EPS) * gain + sh).astype(BF16)
        return carry

    lax.fori_loop(0, tm // _NORM_CHUNK, body, 0)


def _qkv_kernel(x_ref, mod_ref, ln_ref, w_ref, gain_ref, cos_ref, sa_ref, sb_ref,
                o_ref, h_s, acc_s):
    j = pl.program_id(2)
    scale = HEAD_DIM ** -0.5
    nh = o_ref.shape[0]

    @pl.when(j == 0)
    def _():
        _norm_modulate(x_ref, ln_ref, mod_ref[1:2, :], mod_ref[0:1, :], h_s)

    acc_s[...] = jnp.dot(h_s[...], w_ref[...], preferred_element_type=F32)

    def qk_epilogue(gain_row, out_scale, rope):
        gain = gain_ref[gain_row:gain_row + 1, :] * out_scale
        for h in range(nh):
            a = acc_s[:, h * HEAD_DIM:(h + 1) * HEAD_DIM]
            ms = jnp.mean(a * a, axis=-1, keepdims=True)
            y = a * lax.rsqrt(ms + EPS)
            if rope:
                y = (y * cos_ref[...]
                     + pltpu.roll(y, ROT_DIM // 2, 1) * sa_ref[...]
                     + pltpu.roll(y, HEAD_DIM - ROT_DIM // 2, 1) * sb_ref[...])
            o_ref[h] = (y * gain).astype(BF16)

    @pl.when(j == 0)
    def _():
        qk_epilogue(0, scale, True)

    @pl.when(j == 1)
    def _():
        qk_epilogue(1, 1.0, True)

    @pl.when(j == 3)
    def _():
        qk_epilogue(2, scale, False)

    @pl.when(j == 4)
    def _():
        qk_epilogue(3, 1.0, False)

    @pl.when((j == 2) | (j == 5))
    def _():
        for h in range(nh):
            o_ref[h] = acc_s[:, h * HEAD_DIM:(h + 1) * HEAD_DIM].astype(BF16)


def _rope_tables(seq):
    half = ROT_DIM // 2
    pos = np.arange(seq, dtype=np.float64)[:, None]
    inv = ROPE_THETA ** (-np.arange(0, ROT_DIM, 2, dtype=np.float64) / ROT_DIM)[None, :]
    ang = pos * inv
    cos = np.ones((seq, HEAD_DIM), np.float64)
    sa = np.zeros((seq, HEAD_DIM), np.float64)
    sb = np.zeros((seq, HEAD_DIM), np.float64)
    cos[:, :half] = np.cos(ang)
    cos[:, half:ROT_DIM] = np.cos(ang)
    sa[:, half:ROT_DIM] = np.sin(ang)
    sb[:, :half] = -np.sin(ang)
    return (jnp.asarray(cos, F32), jnp.asarray(sa, F32), jnp.asarray(sb, F32))


def _qkv_proj(x, mod, ln, w, gains, tables, tm=512):
    bsz, seq, d = x.shape
    n = w.shape[1]
    tn = N_HEADS_DIL * HEAD_DIM
    ng = n // tn
    cos, sa, sb = tables
    tab_spec = pl.BlockSpec((tm, HEAD_DIM), lambda b, i, j: (i, 0))
    return pl.pallas_call(
        _qkv_kernel,
        grid=(bsz, seq // tm, ng),
        in_specs=[
            pl.BlockSpec((None, tm, d), lambda b, i, j: (b, i, 0)),
            pl.BlockSpec((None, 6, d), lambda b, i, j: (b, 0, 0)),
            pl.BlockSpec((1, d), lambda b, i, j: (0, 0)),
            pl.BlockSpec((d, tn), lambda b, i, j: (0, j)),
            pl.BlockSpec((4, HEAD_DIM), lambda b, i, j: (0, 0)),
            tab_spec, tab_spec, tab_spec,
        ],
        out_specs=pl.BlockSpec((None, tn // HEAD_DIM, tm, HEAD_DIM), lambda b, i, j: (b, j, i, 0)),
        out_shape=jax.ShapeDtypeStruct((bsz, n // HEAD_DIM, seq, HEAD_DIM), BF16),
        scratch_shapes=[pltpu.VMEM((tm, d), BF16), pltpu.VMEM((tm, tn), F32)],
        compiler_params=_cparams(("arbitrary", "arbitrary", "arbitrary")),
        name="qkv_proj",
    )(x, mod, ln, w, gains, cos, sa, sb)


_CAST_CHUNK = 256


def _dil_kernel(q_ref, k_ref, v_ref, o_ref, q32, k32, v32, acc_s, m_s, l_s, *, tq):
    seq = q_ref.shape[0]
    kw = tq + 2 * DIL_HALF

    def cast_body(c, carry):
        r0 = pl.multiple_of(c * _CAST_CHUNK, _CAST_CHUNK)
        rows = pl.ds(r0, _CAST_CHUNK)
        q32[rows, :] = q_ref[rows, :].astype(F32)
        k32[rows, :] = k_ref[rows, :].astype(F32)
        v32[rows, :] = v_ref[rows, :].astype(F32)
        return carry

    lax.fori_loop(0, seq // _CAST_CHUNK, cast_body, 0)

    def run_pattern(dil, mode):
        cls_len = seq // dil
        nblk = cls_len // tq

        def body(it, carry):
            r = it // nblk
            n0 = (it % nblk) * tq
            nk0 = jnp.clip(n0 - DIL_HALF, 0, cls_len - kw)
            if dil == 1:
                qrows = pl.ds(pl.multiple_of(n0, tq), tq)
                krows = pl.ds(pl.multiple_of(nk0, 8), kw)
            else:
                qrows = pl.ds(r + dil * n0, tq, stride=dil)
                krows = pl.ds(r + dil * nk0, kw, stride=dil)
            q = q32[qrows, :].astype(BF16)
            k = k32[krows, :].astype(BF16)
            v = v32[krows, :].astype(BF16)
            s = lax.dot_general(q, k, (((1,), (1,)), ((), ())), preferred_element_type=F32)
            qn = n0 + lax.broadcasted_iota(jnp.int32, (tq, kw), 0)
            kn = nk0 + lax.broadcasted_iota(jnp.int32, (tq, kw), 1)
            s = jnp.where(jnp.abs(kn - qn) <= DIL_HALF, s, NEG_INF)
            m = jnp.max(s, axis=-1, keepdims=True)
            p = jnp.exp(s - m)
            l = jnp.sum(p, axis=-1, keepdims=True)
            o = jnp.dot(p.astype(BF16), v, preferred_element_type=F32)
            mb = jnp.broadcast_to(m, (tq, HEAD_DIM))
            lb = jnp.broadcast_to(l, (tq, HEAD_DIM))
            if mode == "init":
                acc_s[qrows, :] = o
                m_s[qrows, :] = mb
                l_s[qrows, :] = lb
            else:
                m_old = m_s[qrows, :]
                m_new = jnp.maximum(m_old, mb)
                a_old = jnp.exp(m_old - m_new)
                a_new = jnp.exp(mb - m_new)
                acc = acc_s[qrows, :] * a_old + o * a_new
                l_new = l_s[qrows, :] * a_old + lb * a_new
                if mode == "merge":
                    acc_s[qrows, :] = acc
                    m_s[qrows, :] = m_new
                    l_s[qrows, :] = l_new
                else:
                    o_ref[qrows, :] = (acc / l_new).astype(BF16)
            return carry

        lax.fori_loop(0, dil * nblk, body, 0)

    run_pattern(DILATIONS[0], "init")
    run_pattern(DILATIONS[1], "merge")
    run_pattern(DILATIONS[2], "final")


def _dilated_attention(qkv, tq=256):
    bsz, _, seq, hd = qkv.shape
    nh = N_HEADS_DIL

    def spec(off):
        return pl.BlockSpec((None, None, seq, hd), lambda b, h: (b, off + h, 0, 0))

    return pl.pallas_call(
        functools.partial(_dil_kernel, tq=tq),
        grid=(bsz, nh),
        in_specs=[spec(0), spec(nh), spec(2 * nh)],
        out_specs=pl.BlockSpec((None, None, seq, hd), lambda b, h: (b, h, 0, 0)),
        out_shape=jax.ShapeDtypeStruct((bsz, nh, seq, hd), BF16),
        scratch_shapes=[pltpu.VMEM((seq, hd), F32) for _ in range(6)],
        compiler_params=_cparams(("arbitrary", "arbitrary")),
        name="dilated_attention",
    )(qkv, qkv, qkv)


_NA_QROWS = 2
_NA_KROWS = NA_ROWS + _NA_QROWS


def _na_cases(rows):
    return (0, 2, 4, rows - 4, rows - 2)


def _na_bias_tables(rpb, rows):
    nq = _NA_QROWS * GRID_W
    nk = _NA_KROWS * GRID_W
    qi_l = np.arange(nq) // GRID_W
    qc = np.arange(nq) % GRID_W
    kr_l = np.arange(nk) // GRID_W
    kc = np.arange(nk) % GRID_W
    roffs, coffs, valids = [], [], []
    for i0 in _na_cases(rows):
        ks = int(np.clip(i0 - NA_ROWS // 2, 0, rows - _NA_KROWS))
        qi = i0 + qi_l
        kr = ks + kr_l
        rs = np.clip(qi - NA_ROWS // 2, 0, rows - NA_ROWS)
        cs = np.clip(qc - NA_COLS // 2, 0, GRID_W - NA_COLS)
        rvalid = (kr[None, :] >= rs[:, None]) & (kr[None, :] < rs[:, None] + NA_ROWS)
        cvalid = (kc[None, :] >= cs[:, None]) & (kc[None, :] < cs[:, None] + NA_COLS)
        valids.append(rvalid & cvalid)
        roffs.append(np.clip(kr[None, :] - qi[:, None] + NA_ROWS - 1, 0, 2 * NA_ROWS - 2))
        coffs.append(np.clip(kc[None, :] - qc[:, None] + NA_COLS - 1, 0, 2 * NA_COLS - 2))
    roff = np.stack(roffs)
    coff = np.stack(coffs)
    valid = np.stack(valids)
    bias = rpb.astype(F32)[:, roff, coff]
    return jnp.where(valid[None], bias, NEG_INF)


def _na_kernel(q_ref, k_ref, v_ref, bias_ref, o_ref):
    seq = q_ref.shape[0]
    rows = seq // GRID_W
    nq = _NA_QROWS * GRID_W
    nk = _NA_KROWS * GRID_W

    def body(ib, carry):
        i0 = ib * _NA_QROWS
        ks = jnp.clip(i0 - NA_ROWS // 2, 0, rows - _NA_KROWS)
        case = (i0 - ks) // 2
        q = q_ref[pl.ds(pl.multiple_of(i0 * GRID_W, nq), nq), :]
        krows = pl.ds(pl.multiple_of(ks * GRID_W, GRID_W), nk)
        k = k_ref[krows, :]
        v = v_ref[krows, :]
        s = lax.dot_general(q, k, (((1,), (1,)), ((), ())), preferred_element_type=F32)
        s = s + bias_ref[case]
        m = jnp.max(s, axis=-1, keepdims=True)
        p = jnp.exp(s - m)
        l = jnp.sum(p, axis=-1, keepdims=True)
        o = jnp.dot(p.astype(BF16), v, preferred_element_type=F32)
        o_ref[pl.ds(pl.multiple_of(i0 * GRID_W, nq), nq), :] = (o / l).astype(BF16)
        return carry

    lax.fori_loop(0, rows // _NA_QROWS, body, 0)


def _neighborhood_attention(qkv, bias):
    bsz, _, seq, hd = qkv.shape
    nh = N_HEADS_NA
    base = 3 * N_HEADS_DIL
    ncase, nq, nk = bias.shape[1:]

    def spec(off):
        return pl.BlockSpec((None, None, seq, hd), lambda b, h: (b, base + off + h, 0, 0))

    return pl.pallas_call(
        _na_kernel,
        grid=(bsz, nh),
        in_specs=[spec(0), spec(nh), spec(2 * nh),
                  pl.BlockSpec((None, ncase, nq, nk), lambda b, h: (h, 0, 0, 0))],
        out_specs=pl.BlockSpec((None, None, seq, hd), lambda b, h: (b, h, 0, 0)),
        out_shape=jax.ShapeDtypeStruct((bsz, nh, seq, hd), BF16),
        compiler_params=_cparams(("arbitrary", "arbitrary")),
        name="neighborhood_attention",
    )(qkv, qkv, qkv, bias)


def _out_kernel(a_ref, b_ref, x_ref, mod_ref, w_ref, o_ref, mix_s):
    j = pl.program_id(2)
    na = a_ref.shape[0]

    @pl.when(j == 0)
    def _():
        for h in range(na):
            mix_s[:, h * HEAD_DIM:(h + 1) * HEAD_DIM] = a_ref[h]
        for h in range(b_ref.shape[0]):
            mix_s[:, (na + h) * HEAD_DIM:(na + h + 1) * HEAD_DIM] = b_ref[h]

    y = jnp.dot(mix_s[...], w_ref[...], preferred_element_type=F32)
    o_ref[...] = x_ref[...] + mod_ref[2:3, :] * y


def _out_proj(att_a, att_b, x, mod, w, tm=512, tn=1024):
    bsz, seq, d = x.shape
    na, nb = att_a.shape[1], att_b.shape[1]
    return pl.pallas_call(
        _out_kernel,
        grid=(bsz, seq // tm, d // tn),
        in_specs=[
            pl.BlockSpec((None, na, tm, HEAD_DIM), lambda b, i, j: (b, 0, i, 0)),
            pl.BlockSpec((None, nb, tm, HEAD_DIM), lambda b, i, j: (b, 0, i, 0)),
            pl.BlockSpec((None, tm, tn), lambda b, i, j: (b, i, j)),
            pl.BlockSpec((None, 6, tn), lambda b, i, j: (b, 0, j)),
            pl.BlockSpec((d, tn), lambda b, i, j: (0, j)),
        ],
        out_specs=pl.BlockSpec((None, tm, tn), lambda b, i, j: (b, i, j)),
        out_shape=jax.ShapeDtypeStruct((bsz, seq, d), F32),
        scratch_shapes=[pltpu.VMEM((tm, d), BF16)],
        compiler_params=_cparams(("arbitrary", "arbitrary", "arbitrary")),
        name="attn_out_proj",
    )(att_a, att_b, x, mod, w)


def _mlp_in_kernel(x_ref, mod_ref, ln_ref, w_ref, o_ref, h_s):
    @pl.when(pl.program_id(2) == 0)
    def _():
        _norm_modulate(x_ref, ln_ref, mod_ref[4:5, :], mod_ref[3:4, :], h_s)

    y = jnp.maximum(jnp.dot(h_s[...], w_ref[...], preferred_element_type=F32), 0.0)
    o_ref[...] = (y * y).astype(BF16)


def _mlp_in(x, mod, ln, w, tm=512, tn=1024):
    bsz, seq, d = x.shape
    f = w.shape[1]
    return pl.pallas_call(
        _mlp_in_kernel,
        grid=(bsz, seq // tm, f // tn),
        in_specs=[
            pl.BlockSpec((None, tm, d), lambda b, i, j: (b, i, 0)),
            pl.BlockSpec((None, 6, d), lambda b, i, j: (b, 0, 0)),
            pl.BlockSpec((1, d), lambda b, i, j: (0, 0)),
            pl.BlockSpec((d, tn), lambda b, i, j: (0, j)),
        ],
        out_specs=pl.BlockSpec((None, tm, tn), lambda b, i, j: (b, i, j)),
        out_shape=jax.ShapeDtypeStruct((bsz, seq, f), BF16),
        scratch_shapes=[pltpu.VMEM((tm, d), BF16)],
        compiler_params=_cparams(("arbitrary", "arbitrary", "arbitrary")),
        name="mlp_in",
    )(x, mod, ln, w)


def _mlp_out_kernel(h_ref, x_ref, mod_ref, w_ref, o_ref):
    y = jnp.dot(h_ref[...], w_ref[...], preferred_element_type=F32)
    o_ref[...] = x_ref[...] + mod_ref[5:6, :] * y


def _mlp_out(hid, x, mod, w, tm=512, tn=512):
    bsz, seq, d = x.shape
    f = hid.shape[2]
    return pl.pallas_call(
        _mlp_out_kernel,
        grid=(bsz, seq // tm, d // tn),
        in_specs=[
            pl.BlockSpec((None, tm, f), lambda b, i, j: (b, i, 0)),
            pl.BlockSpec((None, tm, tn), lambda b, i, j: (b, i, j)),
            pl.BlockSpec((None, 6, tn), lambda b, i, j: (b, 0, j)),
            pl.BlockSpec((f, tn), lambda b, i, j: (0, j)),
        ],
        out_specs=pl.BlockSpec((None, tm, tn), lambda b, i, j: (b, i, j)),
        out_shape=jax.ShapeDtypeStruct((bsz, seq, d), F32),
        compiler_params=_cparams(("arbitrary", "arbitrary", "arbitrary")),
        name="mlp_out",
    )(hid, x, mod, w)


def _forward(x, c, ln1, w_ada, b_ada, w_in, q_norm_dil, k_norm_dil, q_norm_na, k_norm_na,
             na_rel_bias, w_out, ln2, w_mlp_in, w_mlp_out, dil_tq):
    bsz, seq, d = x.shape
    depth = w_ada.shape[0]
    rows = seq // GRID_W
    tables = _rope_tables(seq)
    mod_all = _modulation(c, w_ada, b_ada).reshape(depth, bsz, 6, d)
    for layer in range(depth):
        mod = mod_all[layer]
        gains = jnp.stack([q_norm_dil[layer], k_norm_dil[layer],
                           q_norm_na[layer], k_norm_na[layer]]).astype(F32)
        qkv = _qkv_proj(x, mod, ln1[layer][None, :], w_in[layer].astype(BF16), gains, tables)
        att_dil = _dilated_attention(qkv, tq=dil_tq)
        att_na = _neighborhood_attention(qkv, _na_bias_tables(na_rel_bias[layer], rows))
        x = _out_proj(att_dil, att_na, x, mod, w_out[layer].astype(BF16))
        hid = _mlp_in(x, mod, ln2[layer][None, :], w_mlp_in[layer].astype(BF16))
        x = _mlp_out(hid, x, mod, w_mlp_out[layer].astype(BF16))
    return x


def kernel(x, c, ln1, w_ada, b_ada, w_in, q_norm_dil, k_norm_dil, q_norm_na, k_norm_na,
           na_rel_bias, w_out, ln2, w_mlp_in, w_mlp_out):
    return _forward(x, c, ln1, w_ada, b_ada, w_in, q_norm_dil, k_norm_dil, q_norm_na,
                    k_norm_na, na_rel_bias, w_out, ln2, w_mlp_in, w_mlp_out, dil_tq=256)
```

```python
import functools

import numpy as np
import jax
import jax.numpy as jnp
from jax import lax
from jax.experimental import pallas as pl
from jax.experimental.pallas import tpu as pltpu

HEAD_DIM = 128
N_HEADS_DIL = 8
N_HEADS_NA = 8
ROT_DIM = HEAD_DIM // 4
ROPE_THETA = 500000.0
DIL_HALF = 64
DILATIONS = (16, 4, 1)
GRID_W = 64
NA_ROWS = 8
NA_COLS = 16
EPS = 1e-6
NEG_INF = -1e30

V7X_VMEM_BYTES = 64 * 1024 * 1024
VMEM_LIMIT = 56 * 1024 * 1024

F32 = jnp.float32
BF16 = jnp.bfloat16


def _cparams(sem):
    return pltpu.CompilerParams(dimension_semantics=sem, vmem_limit_bytes=VMEM_LIMIT)


def _mod_kernel(ct_ref, w_ref, b_ref, o_ref, act_s, *, nb):
    d = w_ref.shape[0]
    tn = w_ref.shape[1]

    @pl.when((pl.program_id(0) == 0) & (pl.program_id(1) == 0))
    def _():
        ct = ct_ref[...]
        act = ct * (1.0 / (1.0 + jnp.exp(-ct)))
        for b in range(nb):
            act_s[b] = jnp.broadcast_to(act[:, b:b + 1], (d, HEAD_DIM))

    def body(k, accs):
        r0 = pl.multiple_of(k * 8, 8)
        w8 = w_ref[pl.ds(r0, 8), :]
        out = []
        for b, acc in enumerate(accs):
            a = act_s[b, pl.ds(r0, 8), :]
            out.append(acc + w8 * jnp.concatenate([a] * (tn // HEAD_DIM), axis=1))
        return tuple(out)

    accs = lax.fori_loop(0, d // 8, body, tuple(jnp.zeros((8, tn), F32) for _ in range(nb)),
                         unroll=4)
    for b in range(nb):
        o_ref[b:b + 1, :] = jnp.sum(accs[b], axis=0, keepdims=True) + b_ref[...]


def _modulation(c, w_ada, b_ada):
    nl, d, n6 = w_ada.shape
    nb = c.shape[0]
    tn = 1024
    return pl.pallas_call(
        functools.partial(_mod_kernel, nb=nb),
        grid=(nl, n6 // tn),
        in_specs=[
            pl.BlockSpec((d, nb), lambda l, j: (0, 0)),
            pl.BlockSpec((None, d, tn), lambda l, j: (l, 0, j)),
            pl.BlockSpec((None, 1, tn), lambda l, j: (l, 0, j)),
        ],
        out_specs=pl.BlockSpec((None, nb, tn), lambda l, j: (l, 0, j)),
        out_shape=jax.ShapeDtypeStruct((nl, nb, n6), F32),
        scratch_shapes=[pltpu.VMEM((nb, d, HEAD_DIM), F32)],
        compiler_params=_cparams(("arbitrary", "arbitrary")),
        name="adaln_modulation",
    )(c.T, w_ada, b_ada.reshape(nl, 1, n6))


_NORM_CHUNK = 64


def _norm_modulate(x_ref, ln_ref, sc, sh, h_s):
    tm = x_ref.shape[0]
    gain = ln_ref[...] * (1.0 + sc)

    def body(c, carry):
        r0 = pl.multiple_of(c * _NORM_CHUNK, _NORM_CHUNK)
        x = x_ref[pl.ds(r0, _NORM_CHUNK), :]
        ms = jnp.mean(x * x, axis=-1, keepdims=True)
        h_s[pl.ds(r0, _NORM_CHUNK), :] = (x * lax.rsqrt(ms + EPS) * gain + sh).astype(BF16)
        return carry

    lax.fori_loop(0, tm // _NORM_CHUNK, body, 0)


def _qkv_kernel(x_ref, mod_ref, ln_ref, w_ref, gain_ref, cos_ref, sa_ref, sb_ref,
                o_ref, h_s, acc_s):
    j = pl.program_id(2)
    scale = HEAD_DIM ** -0.5
    nh = o_ref.shape[0]

    @pl.when(j == 0)
    def _():
        _norm_modulate(x_ref, ln_ref, mod_ref[1:2, :], mod_ref[0:1, :], h_s)

    acc_s[...] = jnp.dot(h_s[...], w_ref[...], preferred_element_type=F32)

    def qk_epilogue(gain_row, out_scale, rope):
        gain = gain_ref[gain_row:gain_row + 1, :] * out_scale
        for h in range(nh):
            a = acc_s[:, h * HEAD_DIM:(h + 1) * HEAD_DIM]
            ms = jnp.mean(a * a, axis=-1, keepdims=True)
            y = a * lax.rsqrt(ms + EPS)
            if rope:
                y = (y * cos_ref[...]
                     + pltpu.roll(y, ROT_DIM // 2, 1) * sa_ref[...]
                     + pltpu.roll(y, HEAD_DIM - ROT_DIM // 2, 1) * sb_ref[...])
            o_ref[h] = (y * gain).astype(BF16)

    @pl.when(j == 0)
    def _():
        qk_epilogue(0, scale, True)

    @pl.when(j == 1)
    def _():
        qk_epilogue(1, 1.0, True)

    @pl.when(j == 3)
    def _():
        qk_epilogue(2, scale, False)

    @pl.when(j == 4)
    def _():
        qk_epilogue(3, 1.0, False)

    @pl.when((j == 2) | (j == 5))
    def _():
        for h in range(nh):
            o_ref[h] = acc_s[:, h * HEAD_DIM:(h + 1) * HEAD_DIM].astype(BF16)


def _rope_tables(seq):
    half = ROT_DIM // 2
    pos = np.arange(seq, dtype=np.float64)[:, None]
    inv = ROPE_THETA ** (-np.arange(0, ROT_DIM, 2, dtype=np.float64) / ROT_DIM)[None, :]
    ang = pos * inv
    cos = np.ones((seq, HEAD_DIM), np.float64)
    sa = np.zeros((seq, HEAD_DIM), np.float64)
    sb = np.zeros((seq, HEAD_DIM), np.float64)
    cos[:, :half] = np.cos(ang)
    cos[:, half:ROT_DIM] = np.cos(ang)
    sa[:, half:ROT_DIM] = np.sin(ang)
    sb[:, :half] = -np.sin(ang)
    return (jnp.asarray(cos, F32), jnp.asarray(sa, F32), jnp.asarray(sb, F32))


def _qkv_proj(x, mod, ln, w, gains, tables, tm=512):
    bsz, seq, d = x.shape
    n = w.shape[1]
    tn = N_HEADS_DIL * HEAD_DIM
    ng = n // tn
    cos, sa, sb = tables
    tab_spec = pl.BlockSpec((tm, HEAD_DIM), lambda b, i, j: (i, 0))
    return pl.pallas_call(
        _qkv_kernel,
        grid=(bsz, seq // tm, ng),
        in_specs=[
            pl.BlockSpec((None, tm, d), lambda b, i, j: (b, i, 0)),
            pl.BlockSpec((None, 6, d), lambda b, i, j: (b, 0, 0)),
            pl.BlockSpec((1, d), lambda b, i, j: (0, 0)),
            pl.BlockSpec((d, tn), lambda b, i, j: (0, j)),
            pl.BlockSpec((4, HEAD_DIM), lambda b, i, j: (0, 0)),
            tab_spec, tab_spec, tab_spec,
        ],
        out_specs=pl.BlockSpec((None, tn // HEAD_DIM, tm, HEAD_DIM), lambda b, i, j: (b, j, i, 0)),
        out_shape=jax.ShapeDtypeStruct((bsz, n // HEAD_DIM, seq, HEAD_DIM), BF16),
        scratch_shapes=[pltpu.VMEM((tm, d), BF16), pltpu.VMEM((tm, tn), F32)],
        compiler_params=_cparams(("arbitrary", "arbitrary", "arbitrary")),
        name="qkv_proj",
    )(x, mod, ln, w, gains, cos, sa, sb)


_CAST_CHUNK = 256


def _dil_mask_tables(tq):
    kw = tq + 2 * DIL_HALF
    i = np.arange(tq)[:, None]
    j = np.arange(kw)[None, :]
    tabs = [np.where(np.abs(j - i - DIL_HALF * case) <= DIL_HALF, 0.0, NEG_INF) for case in range(3)]
    return jnp.asarray(np.stack(tabs), F32)


def _dil_kernel(q_ref, k_ref, v_ref, mask_ref, o_ref, q32, k32, v32, acc_s, m_s, l_s, *, tq, unroll):
    seq = q_ref.shape[0]
    kw = tq + 2 * DIL_HALF

    def cast_body(c, carry):
        r0 = pl.multiple_of(c * _CAST_CHUNK, _CAST_CHUNK)
        rows = pl.ds(r0, _CAST_CHUNK)
        q32[rows, :] = q_ref[rows, :].astype(F32)
        k32[rows, :] = k_ref[rows, :].astype(F32)
        v32[rows, :] = v_ref[rows, :].astype(F32)
        return carry

    lax.fori_loop(0, seq // _CAST_CHUNK, cast_body, 0)

    def run_pattern(dil, mode):
        cls_len = seq // dil
        nblk = cls_len // tq

        def body(it, carry):
            r = it // nblk
            n0 = (it % nblk) * tq
            nk0 = jnp.clip(n0 - DIL_HALF, 0, cls_len - kw)
            if dil == 1:
                qrows = pl.ds(pl.multiple_of(n0, tq), tq)
                krows = pl.ds(pl.multiple_of(nk0, 8), kw)
            else:
                qrows = pl.ds(r + dil * n0, tq, stride=dil)
                krows = pl.ds(r + dil * nk0, kw, stride=dil)
            q = q32[qrows, :].astype(BF16)
            k = k32[krows, :].astype(BF16)
            v = v32[krows, :].astype(BF16)
            s = lax.dot_general(q, k, (((1,), (1,)), ((), ())), preferred_element_type=F32)
            s = s + mask_ref[(n0 - nk0) // DIL_HALF]
            m = jnp.max(s, axis=-1, keepdims=True)
            p = jnp.exp(s - m)
            l = jnp.sum(p, axis=-1, keepdims=True)
            o = jnp.dot(p.astype(BF16), v, preferred_element_type=F32)
            mb = jnp.broadcast_to(m, (tq, HEAD_DIM))
            lb = jnp.broadcast_to(l, (tq, HEAD_DIM))
            if mode == "init":
                acc_s[qrows, :] = o
                m_s[qrows, :] = mb
                l_s[qrows, :] = lb
            else:
                m_old = m_s[qrows, :]
                m_new = jnp.maximum(m_old, mb)
                a_old = jnp.exp(m_old - m_new)
                a_new = jnp.exp(mb - m_new)
                acc = acc_s[qrows, :] * a_old + o * a_new
                l_new = l_s[qrows, :] * a_old + lb * a_new
                if mode == "merge":
                    acc_s[qrows, :] = acc
                    m_s[qrows, :] = m_new
                    l_s[qrows, :] = l_new
                else:
                    o_ref[qrows, :] = (acc / l_new).astype(BF16)
            return carry

        lax.fori_loop(0, dil * nblk, body, 0, unroll=unroll)

    run_pattern(DILATIONS[0], "init")
    run_pattern(DILATIONS[1], "merge")
    run_pattern(DILATIONS[2], "final")


def _dilated_attention(qkv, tq=256, unroll=2):
    bsz, _, seq, hd = qkv.shape
    nh = N_HEADS_DIL
    masks = _dil_mask_tables(tq)

    def spec(off):
        return pl.BlockSpec((None, None, seq, hd), lambda b, h: (b, off + h, 0, 0))

    return pl.pallas_call(
        functools.partial(_dil_kernel, tq=tq, unroll=unroll),
        grid=(bsz, nh),
        in_specs=[spec(0), spec(nh), spec(2 * nh),
                  pl.BlockSpec(masks.shape, lambda b, h: (0, 0, 0))],
        out_specs=pl.BlockSpec((None, None, seq, hd), lambda b, h: (b, h, 0, 0)),
        out_shape=jax.ShapeDtypeStruct((bsz, nh, seq, hd), BF16),
        scratch_shapes=[pltpu.VMEM((seq, hd), F32) for _ in range(6)],
        compiler_params=_cparams(("arbitrary", "arbitrary")),
        name="dilated_attention",
    )(qkv, qkv, qkv, masks)


_NA_QROWS = 2
_NA_KROWS = NA_ROWS + _NA_QROWS
_NA_UNROLL = 2


def _na_cases(rows):
    return (0, 2, 4, rows - 4, rows - 2)


def _na_bias_tables(rpb, rows):
    nq = _NA_QROWS * GRID_W
    nk = _NA_KROWS * GRID_W
    nh, nrow, ncol = rpb.shape
    w = GRID_W
    u = jnp.pad(rpb.astype(F32), ((0, 0), (0, 0), (w - NA_COLS, w + NA_COLS - ncol)))
    toep = jnp.broadcast_to(u[:, :, None, :], (nh, nrow, w, 2 * w)).reshape(nh, nrow, 2 * w * w)
    toep = toep[:, :, :w * (2 * w - 1)].reshape(nh, nrow, w, 2 * w - 1)[:, :, :, w - 1:]
    cs = np.clip(np.arange(w) - NA_COLS // 2, 0, w - NA_COLS)
    cvalid = (np.arange(w)[None, :] >= cs[:, None]) & (np.arange(w)[None, :] < cs[:, None] + NA_COLS)
    slabs, rvalids = [], []
    for i0 in _na_cases(rows):
        ks = int(np.clip(i0 - NA_ROWS // 2, 0, rows - _NA_KROWS))
        for qi in range(i0, i0 + _NA_QROWS):
            rs = int(np.clip(qi - NA_ROWS // 2, 0, rows - NA_ROWS))
            for kr in range(ks, ks + _NA_KROWS):
                slabs.append(toep[:, int(np.clip(kr - qi + NA_ROWS - 1, 0, nrow - 1))])
                rvalids.append(rs <= kr < rs + NA_ROWS)
    ncase = len(_na_cases(rows))
    tab = jnp.stack(slabs, axis=1).reshape(nh, ncase, _NA_QROWS, _NA_KROWS, w, w)
    rvalid = np.asarray(rvalids).reshape(ncase, _NA_QROWS, _NA_KROWS)
    valid = rvalid[:, :, :, None, None] & cvalid[None, None, None]
    tab = jnp.where(valid[None], tab, NEG_INF)
    return tab.transpose(0, 1, 2, 4, 3, 5).reshape(nh, ncase, nq, nk)


def _na_kernel(q_ref, k_ref, v_ref, bias_ref, o_ref):
    seq = q_ref.shape[0]
    rows = seq // GRID_W
    nq = _NA_QROWS * GRID_W
    nk = _NA_KROWS * GRID_W

    def body(ib, carry):
        i0 = ib * _NA_QROWS
        ks = jnp.clip(i0 - NA_ROWS // 2, 0, rows - _NA_KROWS)
        case = (i0 - ks) // 2
        q = q_ref[pl.ds(pl.multiple_of(i0 * GRID_W, nq), nq), :]
        krows = pl.ds(pl.multiple_of(ks * GRID_W, GRID_W), nk)
        k = k_ref[krows, :]
        v = v_ref[krows, :]
        s = lax.dot_general(q, k, (((1,), (1,)), ((), ())), preferred_element_type=F32)
        s = s + bias_ref[case]
        m = jnp.max(s, axis=-1, keepdims=True)
        p = jnp.exp(s - m)
        l = jnp.sum(p, axis=-1, keepdims=True)
        o = jnp.dot(p.astype(BF16), v, preferred_element_type=F32)
        o_ref[pl.ds(pl.multiple_of(i0 * GRID_W, nq), nq), :] = (o / l).astype(BF16)
        return carry

    lax.fori_loop(0, rows // _NA_QROWS, body, 0, unroll=_NA_UNROLL)


def _neighborhood_attention(qkv, bias):
    bsz, _, seq, hd = qkv.shape
    nh = N_HEADS_NA
    base = 3 * N_HEADS_DIL
    ncase, nq, nk = bias.shape[1:]

    def spec(off):
        return pl.BlockSpec((None, None, seq, hd), lambda b, h: (b, base + off + h, 0, 0))

    return pl.pallas_call(
        _na_kernel,
        grid=(bsz, nh),
        in_specs=[spec(0), spec(nh), spec(2 * nh),
                  pl.BlockSpec((None, ncase, nq, nk), lambda b, h: (h, 0, 0, 0))],
        out_specs=pl.BlockSpec((None, None, seq, hd), lambda b, h: (b, h, 0, 0)),
        out_shape=jax.ShapeDtypeStruct((bsz, nh, seq, hd), BF16),
        compiler_params=_cparams(("arbitrary", "arbitrary")),
        name="neighborhood_attention",
    )(qkv, qkv, qkv, bias)


def _out_kernel(a_ref, b_ref, x_ref, mod_ref, w_ref, o_ref, mix_s):
    j = pl.program_id(2)
    na = a_ref.shape[0]

    @pl.when(j == 0)
    def _():
        for h in range(na):
            mix_s[:, h * HEAD_DIM:(h + 1) * HEAD_DIM] = a_ref[h]
        for h in range(b_ref.shape[0]):
            mix_s[:, (na + h) * HEAD_DIM:(na + h + 1) * HEAD_DIM] = b_ref[h]

    y = jnp.dot(mix_s[...], w_ref[...], preferred_element_type=F32)
    o_ref[...] = x_ref[...] + mod_ref[2:3, :] * y


def _out_proj(att_a, att_b, x, mod, w, tm=512, tn=1024):
    bsz, seq, d = x.shape
    na, nb = att_a.shape[1], att_b.shape[1]
    return pl.pallas_call(
        _out_kernel,
        grid=(bsz, seq // tm, d // tn),
        in_specs=[
            pl.BlockSpec((None, na, tm, HEAD_DIM), lambda b, i, j: (b, 0, i, 0)),
            pl.BlockSpec((None, nb, tm, HEAD_DIM), lambda b, i, j: (b, 0, i, 0)),
            pl.BlockSpec((None, tm, tn), lambda b, i, j: (b, i, j)),
            pl.BlockSpec((None, 6, tn), lambda b, i, j: (b, 0, j)),
            pl.BlockSpec((d, tn), lambda b, i, j: (0, j)),
        ],
        out_specs=pl.BlockSpec((None, tm, tn), lambda b, i, j: (b, i, j)),
        out_shape=jax.ShapeDtypeStruct((bsz, seq, d), F32),
        scratch_shapes=[pltpu.VMEM((tm, d), BF16)],
        compiler_params=_cparams(("arbitrary", "arbitrary", "arbitrary")),
        name="attn_out_proj",
    )(att_a, att_b, x, mod, w)


def _mlp_in_kernel(x_ref, mod_ref, ln_ref, w_ref, o_ref, h_s):
    @pl.when(pl.program_id(2) == 0)
    def _():
        _norm_modulate(x_ref, ln_ref, mod_ref[4:5, :], mod_ref[3:4, :], h_s)

    y = jnp.maximum(jnp.dot(h_s[...], w_ref[...], preferred_element_type=F32), 0.0)
    o_ref[...] = (y * y).astype(BF16)


def _mlp_in(x, mod, ln, w, tm=512, tn=1024):
    bsz, seq, d = x.shape
    f = w.shape[1]
    return pl.pallas_call(
        _mlp_in_kernel,
        grid=(bsz, seq // tm, f // tn),
        in_specs=[
            pl.BlockSpec((None, tm, d), lambda b, i, j: (b, i, 0)),
            pl.BlockSpec((None, 6, d), lambda b, i, j: (b, 0, 0)),
            pl.BlockSpec((1, d), lambda b, i, j: (0, 0)),
            pl.BlockSpec((d, tn), lambda b, i, j: (0, j)),
        ],
        out_specs=pl.BlockSpec((None, tm, tn), lambda b, i, j: (b, i, j)),
        out_shape=jax.ShapeDtypeStruct((bsz, seq, f), BF16),
        scratch_shapes=[pltpu.VMEM((tm, d), BF16)],
        compiler_params=_cparams(("arbitrary", "arbitrary", "arbitrary")),
        name="mlp_in",
    )(x, mod, ln, w)


def _mlp_out_kernel(h_ref, x_ref, mod_ref, w_ref, o_ref):
    y = jnp.dot(h_ref[...], w_ref[...], preferred_element_type=F32)
    o_ref[...] = x_ref[...] + mod_ref[5:6, :] * y


def _mlp_out(hid, x, mod, w, tm=512, tn=512):
    bsz, seq, d = x.shape
    f = hid.shape[2]
    return pl.pallas_call(
        _mlp_out_kernel,
        grid=(bsz, seq // tm, d // tn),
        in_specs=[
            pl.BlockSpec((None, tm, f), lambda b, i, j: (b, i, 0)),
            pl.BlockSpec((None, tm, tn), lambda b, i, j: (b, i, j)),
            pl.BlockSpec((None, 6, tn), lambda b, i, j: (b, 0, j)),
            pl.BlockSpec((f, tn), lambda b, i, j: (0, j)),
        ],
        out_specs=pl.BlockSpec((None, tm, tn), lambda b, i, j: (b, i, j)),
        out_shape=jax.ShapeDtypeStruct((bsz, seq, d), F32),
        compiler_params=_cparams(("arbitrary", "arbitrary", "arbitrary")),
        name="mlp_out",
    )(hid, x, mod, w)


def _forward(x, c, ln1, w_ada, b_ada, w_in, q_norm_dil, k_norm_dil, q_norm_na, k_norm_na,
             na_rel_bias, w_out, ln2, w_mlp_in, w_mlp_out, dil_tq):
    bsz, seq, d = x.shape
    depth = w_ada.shape[0]
    rows = seq // GRID_W
    tables = _rope_tables(seq)
    mod_all = _modulation(c, w_ada, b_ada).reshape(depth, bsz, 6, d)
    for layer in range(depth):
        mod = mod_all[layer]
        gains = jnp.stack([q_norm_dil[layer], k_norm_dil[layer],
                           q_norm_na[layer], k_norm_na[layer]]).astype(F32)
        qkv = _qkv_proj(x, mod, ln1[layer][None, :], w_in[layer].astype(BF16), gains, tables)
        att_dil = _dilated_attention(qkv, tq=dil_tq)
        att_na = _neighborhood_attention(qkv, _na_bias_tables(na_rel_bias[layer], rows))
        x = _out_proj(att_dil, att_na, x, mod, w_out[layer].astype(BF16))
        hid = _mlp_in(x, mod, ln2[layer][None, :], w_mlp_in[layer].astype(BF16))
        x = _mlp_out(hid, x, mod, w_mlp_out[layer].astype(BF16))
    return x


def kernel(x, c, ln1, w_ada, b_ada, w_in, q_norm_dil, k_norm_dil, q_norm_na, k_norm_na,
           na_rel_bias, w_out, ln2, w_mlp_in, w_mlp_out):
    return _forward(x, c, ln1, w_ada, b_ada, w_in, q_norm_dil, k_norm_dil, q_norm_na,
                    k_norm_na, na_rel_bias, w_out, ln2, w_mlp_in, w_mlp_out, dil_tq=256)
```

```python
import functools

import numpy as np
import jax
import jax.numpy as jnp
from jax import lax
from jax.experimental import pallas as pl
from jax.experimental.pallas import tpu as pltpu

HEAD_DIM = 128
N_HEADS_DIL = 8
N_HEADS_NA = 8
ROT_DIM = HEAD_DIM // 4
ROPE_THETA = 500000.0
DIL_HALF = 64
DILATIONS = (16, 4, 1)
GRID_W = 64
NA_ROWS = 8
NA_COLS = 16
EPS = 1e-6
NEG_INF = -1e30

V7X_VMEM_BYTES = 64 * 1024 * 1024
VMEM_LIMIT = 56 * 1024 * 1024

F32 = jnp.float32
BF16 = jnp.bfloat16


def _cparams(sem):
    return pltpu.CompilerParams(dimension_semantics=sem, vmem_limit_bytes=VMEM_LIMIT)


def _mod_kernel(ct_ref, w_ref, b_ref, o_ref, act_s, *, nb):
    d = w_ref.shape[0]
    tn = w_ref.shape[1]

    @pl.when((pl.program_id(0) == 0) & (pl.program_id(1) == 0))
    def _():
        ct = ct_ref[...]
        act = ct * (1.0 / (1.0 + jnp.exp(-ct)))
        for b in range(nb):
            act_s[b] = jnp.broadcast_to(act[:, b:b + 1], (d, HEAD_DIM))

    def body(k, accs):
        r0 = pl.multiple_of(k * 8, 8)
        w8 = w_ref[pl.ds(r0, 8), :]
        out = []
        for b, acc in enumerate(accs):
            a = act_s[b, pl.ds(r0, 8), :]
            out.append(acc + w8 * jnp.concatenate([a] * (tn // HEAD_DIM), axis=1))
        return tuple(out)

    accs = lax.fori_loop(0, d // 8, body, tuple(jnp.zeros((8, tn), F32) for _ in range(nb)),
                         unroll=4)
    for b in range(nb):
        o_ref[b:b + 1, :] = jnp.sum(accs[b], axis=0, keepdims=True) + b_ref[...]


def _modulation(c, w_ada, b_ada):
    nl, d, n6 = w_ada.shape
    nb = c.shape[0]
    tn = 1024
    return pl.pallas_call(
        functools.partial(_mod_kernel, nb=nb),
        grid=(nl, n6 // tn),
        in_specs=[
            pl.BlockSpec((d, nb), lambda l, j: (0, 0)),
            pl.BlockSpec((None, d, tn), lambda l, j: (l, 0, j)),
            pl.BlockSpec((None, 1, tn), lambda l, j: (l, 0, j)),
        ],
        out_specs=pl.BlockSpec((None, nb, tn), lambda l, j: (l, 0, j)),
        out_shape=jax.ShapeDtypeStruct((nl, nb, n6), F32),
        scratch_shapes=[pltpu.VMEM((nb, d, HEAD_DIM), F32)],
        compiler_params=_cparams(("arbitrary", "arbitrary")),
        name="adaln_modulation",
    )(c.T, w_ada, b_ada.reshape(nl, 1, n6))


_NORM_CHUNK = 64


def _norm_modulate(x_ref, ln_ref, sc, sh, h_s):
    tm = x_ref.shape[0]
    gain = ln_ref[...] * (1.0 + sc)

    def body(c, carry):
        r0 = pl.multiple_of(c * _NORM_CHUNK, _NORM_CHUNK)
        x = x_ref[pl.ds(r0, _NORM_CHUNK), :]
        ms = jnp.mean(x * x, axis=-1, keepdims=True)
        h_s[pl.ds(r0, _NORM_CHUNK), :] = (x * lax.rsqrt(ms + EPS) * gain + sh).astype(BF16)
        return carry

    lax.fori_loop(0, tm // _NORM_CHUNK, body, 0)


def _qkv_kernel(x_ref, mod_ref, ln_ref, w_ref, gain_ref, cos_ref, sa_ref, sb_ref,
                o_ref, h_s, acc_s):
    scale = HEAD_DIM ** -0.5
    nh = N_HEADS_DIL
    tn = nh * HEAD_DIM
    _norm_modulate(x_ref, ln_ref, mod_ref[1:2, :], mod_ref[0:1, :], h_s)

    for g in range(w_ref.shape[1] // tn):
        acc = acc_s.at[g % 2]
        acc[...] = jnp.dot(h_s[...], w_ref[:, g * tn:(g + 1) * tn], preferred_element_type=F32)
        kind = g % 3
        rope = g < 3
        if kind < 2:
            row = 2 * (g // 3) + kind
            gain = gain_ref[row:row + 1, :] * (scale if kind == 0 else 1.0)
        for h in range(nh):
            a = acc[:, h * HEAD_DIM:(h + 1) * HEAD_DIM]
            if kind == 2:
                o_ref[g * nh + h] = a.astype(BF16)
                continue
            ms = jnp.mean(a * a, axis=-1, keepdims=True)
            y = a * lax.rsqrt(ms + EPS) * gain
            if rope:
                y = (y * cos_ref[...]
                     + pltpu.roll(y, ROT_DIM // 2, 1) * sa_ref[...]
                     + pltpu.roll(y, HEAD_DIM - ROT_DIM // 2, 1) * sb_ref[...])
            o_ref[g * nh + h] = y.astype(BF16)


def _rope_tables(seq):
    half = ROT_DIM // 2
    pos = np.arange(seq, dtype=np.float64)[:, None]
    inv = ROPE_THETA ** (-np.arange(0, ROT_DIM, 2, dtype=np.float64) / ROT_DIM)[None, :]
    ang = pos * inv
    cos = np.ones((seq, HEAD_DIM), np.float64)
    sa = np.zeros((seq, HEAD_DIM), np.float64)
    sb = np.zeros((seq, HEAD_DIM), np.float64)
    cos[:, :half] = np.cos(ang)
    cos[:, half:ROT_DIM] = np.cos(ang)
    sa[:, half:ROT_DIM] = np.sin(ang)
    sb[:, :half] = -np.sin(ang)
    return (jnp.asarray(cos, F32), jnp.asarray(sa, F32), jnp.asarray(sb, F32))


def _qkv_proj(x, mod, ln, w, gains, tables, tm=256):
    bsz, seq, d = x.shape
    n = w.shape[1]
    tn = N_HEADS_DIL * HEAD_DIM
    cos, sa, sb = tables
    tab_spec = pl.BlockSpec((tm, HEAD_DIM), lambda b, i: (i, 0))
    return pl.pallas_call(
        _qkv_kernel,
        grid=(bsz, seq // tm),
        in_specs=[
            pl.BlockSpec((None, tm, d), lambda b, i: (b, i, 0)),
            pl.BlockSpec((None, 6, d), lambda b, i: (b, 0, 0)),
            pl.BlockSpec((1, d), lambda b, i: (0, 0)),
            pl.BlockSpec((d, n), lambda b, i: (0, 0), pipeline_mode=pl.Buffered(1)),
            pl.BlockSpec((4, HEAD_DIM), lambda b, i: (0, 0)),
            tab_spec, tab_spec, tab_spec,
        ],
        out_specs=pl.BlockSpec((None, n // HEAD_DIM, tm, HEAD_DIM), lambda b, i: (b, 0, i, 0)),
        out_shape=jax.ShapeDtypeStruct((bsz, n // HEAD_DIM, seq, HEAD_DIM), BF16),
        scratch_shapes=[pltpu.VMEM((tm, d), BF16), pltpu.VMEM((2, tm, tn), F32)],
        compiler_params=_cparams(("arbitrary", "arbitrary")),
        name="qkv_proj",
    )(x, mod, ln, w, gains, cos, sa, sb)


_CAST_CHUNK = 256


def _dil_mask_tables(tq):
    kw = tq + 2 * DIL_HALF
    i = np.arange(tq)[:, None]
    j = np.arange(kw)[None, :]
    tabs = [np.where(np.abs(j - i - DIL_HALF * case) <= DIL_HALF, 0.0, NEG_INF) for case in range(3)]
    return jnp.asarray(np.stack(tabs), F32)


def _dil_kernel(q_ref, k_ref, v_ref, mask_ref, o_ref, q32, k32, v32, acc_s, m_s, l_s, *, tq, unroll):
    seq = q_ref.shape[0]
    kw = tq + 2 * DIL_HALF

    def cast_body(c, carry):
        r0 = pl.multiple_of(c * _CAST_CHUNK, _CAST_CHUNK)
        rows = pl.ds(r0, _CAST_CHUNK)
        q32[rows, :] = q_ref[rows, :].astype(F32)
        k32[rows, :] = k_ref[rows, :].astype(F32)
        v32[rows, :] = v_ref[rows, :].astype(F32)
        return carry

    lax.fori_loop(0, seq // _CAST_CHUNK, cast_body, 0)

    def run_pattern(dil, mode):
        cls_len = seq // dil
        nblk = cls_len // tq

        def body(it, carry):
            r = it // nblk
            n0 = (it % nblk) * tq
            nk0 = jnp.clip(n0 - DIL_HALF, 0, cls_len - kw)
            if dil == 1:
                qrows = pl.ds(pl.multiple_of(n0, tq), tq)
                krows = pl.ds(pl.multiple_of(nk0, 8), kw)
            else:
                qrows = pl.ds(r + dil * n0, tq, stride=dil)
                krows = pl.ds(r + dil * nk0, kw, stride=dil)
            q = q32[qrows, :].astype(BF16)
            k = k32[krows, :].astype(BF16)
            v = v32[krows, :].astype(BF16)
            s = lax.dot_general(q, k, (((1,), (1,)), ((), ())), preferred_element_type=F32)
            s = s + mask_ref[(n0 - nk0) // DIL_HALF]
            m = jnp.max(s, axis=-1, keepdims=True)
            p = jnp.exp(s - m)
            l = jnp.sum(p, axis=-1, keepdims=True)
            o = jnp.dot(p.astype(BF16), v, preferred_element_type=F32)
            mb = jnp.broadcast_to(m, (tq, HEAD_DIM))
            lb = jnp.broadcast_to(l, (tq, HEAD_DIM))
            if mode == "init":
                acc_s[qrows, :] = o
                m_s[qrows, :] = mb
                l_s[qrows, :] = lb
            else:
                m_old = m_s[qrows, :]
                m_new = jnp.maximum(m_old, mb)
                a_old = jnp.exp(m_old - m_new)
                a_new = jnp.exp(mb - m_new)
                acc = acc_s[qrows, :] * a_old + o * a_new
                l_new = l_s[qrows, :] * a_old + lb * a_new
                if mode == "merge":
                    acc_s[qrows, :] = acc
                    m_s[qrows, :] = m_new
                    l_s[qrows, :] = l_new
                else:
                    o_ref[qrows, :] = (acc / l_new).astype(BF16)
            return carry

        lax.fori_loop(0, dil * nblk, body, 0, unroll=unroll)

    run_pattern(DILATIONS[0], "init")
    run_pattern(DILATIONS[1], "merge")
    run_pattern(DILATIONS[2], "final")


def _dilated_attention(qkv, tq=256, unroll=2):
    bsz, _, seq, hd = qkv.shape
    nh = N_HEADS_DIL
    masks = _dil_mask_tables(tq)

    def spec(off):
        return pl.BlockSpec((None, None, seq, hd), lambda b, h: (b, off + h, 0, 0))

    return pl.pallas_call(
        functools.partial(_dil_kernel, tq=tq, unroll=unroll),
        grid=(bsz, nh),
        in_specs=[spec(0), spec(nh), spec(2 * nh),
                  pl.BlockSpec(masks.shape, lambda b, h: (0, 0, 0))],
        out_specs=pl.BlockSpec((None, None, seq, hd), lambda b, h: (b, h, 0, 0)),
        out_shape=jax.ShapeDtypeStruct((bsz, nh, seq, hd), BF16),
        scratch_shapes=[pltpu.VMEM((seq, hd), F32) for _ in range(6)],
        compiler_params=_cparams(("arbitrary", "arbitrary")),
        name="dilated_attention",
    )(qkv, qkv, qkv, masks)


_NA_QROWS = 2
_NA_KROWS = NA_ROWS + _NA_QROWS
_NA_UNROLL = 2


def _na_cases(rows):
    return (0, 2, 4, rows - 4, rows - 2)


def _na_bias_tables(rpb, rows):
    nq = _NA_QROWS * GRID_W
    nk = _NA_KROWS * GRID_W
    nh, nrow, ncol = rpb.shape
    w = GRID_W
    u = jnp.pad(rpb.astype(F32), ((0, 0), (0, 0), (w - NA_COLS, w + NA_COLS - ncol)))
    toep = jnp.broadcast_to(u[:, :, None, :], (nh, nrow, w, 2 * w)).reshape(nh, nrow, 2 * w * w)
    toep = toep[:, :, :w * (2 * w - 1)].reshape(nh, nrow, w, 2 * w - 1)[:, :, :, w - 1:]
    cs = np.clip(np.arange(w) - NA_COLS // 2, 0, w - NA_COLS)
    cvalid = (np.arange(w)[None, :] >= cs[:, None]) & (np.arange(w)[None, :] < cs[:, None] + NA_COLS)
    slabs, rvalids = [], []
    for i0 in _na_cases(rows):
        ks = int(np.clip(i0 - NA_ROWS // 2, 0, rows - _NA_KROWS))
        for qi in range(i0, i0 + _NA_QROWS):
            rs = int(np.clip(qi - NA_ROWS // 2, 0, rows - NA_ROWS))
            for kr in range(ks, ks + _NA_KROWS):
                slabs.append(toep[:, int(np.clip(kr - qi + NA_ROWS - 1, 0, nrow - 1))])
                rvalids.append(rs <= kr < rs + NA_ROWS)
    ncase = len(_na_cases(rows))
    tab = jnp.stack(slabs, axis=1).reshape(nh, ncase, _NA_QROWS, _NA_KROWS, w, w)
    rvalid = np.asarray(rvalids).reshape(ncase, _NA_QROWS, _NA_KROWS)
    valid = rvalid[:, :, :, None, None] & cvalid[None, None, None]
    tab = jnp.where(valid[None], tab, NEG_INF)
    return tab.transpose(0, 1, 2, 4, 3, 5).reshape(nh, ncase, nq, nk)


def _na_kernel(q_ref, k_ref, v_ref, bias_ref, o_ref):
    seq = q_ref.shape[0]
    rows = seq // GRID_W
    nq = _NA_QROWS * GRID_W
    nk = _NA_KROWS * GRID_W

    def body(ib, carry):
        i0 = ib * _NA_QROWS
        ks = jnp.clip(i0 - NA_ROWS // 2, 0, rows - _NA_KROWS)
        case = (i0 - ks) // 2
        q = q_ref[pl.ds(pl.multiple_of(i0 * GRID_W, nq), nq), :]
        krows = pl.ds(pl.multiple_of(ks * GRID_W, GRID_W), nk)
        k = k_ref[krows, :]
        v = v_ref[krows, :]
        s = lax.dot_general(q, k, (((1,), (1,)), ((), ())), preferred_element_type=F32)
        s = s + bias_ref[case]
        m = jnp.max(s, axis=-1, keepdims=True)
        p = jnp.exp(s - m)
        l = jnp.sum(p, axis=-1, keepdims=True)
        o = jnp.dot(p.astype(BF16), v, preferred_element_type=F32)
        o_ref[pl.ds(pl.multiple_of(i0 * GRID_W, nq), nq), :] = (o / l).astype(BF16)
        return carry

    lax.fori_loop(0, rows // _NA_QROWS, body, 0, unroll=_NA_UNROLL)


def _neighborhood_attention(qkv, bias):
    bsz, _, seq, hd = qkv.shape
    nh = N_HEADS_NA
    base = 3 * N_HEADS_DIL
    ncase, nq, nk = bias.shape[1:]

    def spec(off):
        return pl.BlockSpec((None, None, seq, hd), lambda b, h: (b, base + off + h, 0, 0))

    return pl.pallas_call(
        _na_kernel,
        grid=(bsz, nh),
        in_specs=[spec(0), spec(nh), spec(2 * nh),
                  pl.BlockSpec((None, ncase, nq, nk), lambda b, h: (h, 0, 0, 0))],
        out_specs=pl.BlockSpec((None, None, seq, hd), lambda b, h: (b, h, 0, 0)),
        out_shape=jax.ShapeDtypeStruct((bsz, nh, seq, hd), BF16),
        compiler_params=_cparams(("arbitrary", "arbitrary")),
        name="neighborhood_attention",
    )(qkv, qkv, qkv, bias)


def _out_kernel(a_ref, b_ref, x_ref, mod_ref, w_ref, o_ref, mix_s):
    na = a_ref.shape[0]
    for h in range(na):
        mix_s[:, h * HEAD_DIM:(h + 1) * HEAD_DIM] = a_ref[h]
    for h in range(b_ref.shape[0]):
        mix_s[:, (na + h) * HEAD_DIM:(na + h + 1) * HEAD_DIM] = b_ref[h]
    y = jnp.dot(mix_s[...], w_ref[...], preferred_element_type=F32)
    o_ref[...] = x_ref[...] + mod_ref[2:3, :] * y


def _out_proj(att_a, att_b, x, mod, w, tm=512):
    bsz, seq, d = x.shape
    na, nb = att_a.shape[1], att_b.shape[1]
    return pl.pallas_call(
        _out_kernel,
        grid=(bsz, seq // tm),
        in_specs=[
            pl.BlockSpec((None, na, tm, HEAD_DIM), lambda b, i: (b, 0, i, 0)),
            pl.BlockSpec((None, nb, tm, HEAD_DIM), lambda b, i: (b, 0, i, 0)),
            pl.BlockSpec((None, tm, d), lambda b, i: (b, i, 0)),
            pl.BlockSpec((None, 6, d), lambda b, i: (b, 0, 0)),
            pl.BlockSpec((d, d), lambda b, i: (0, 0), pipeline_mode=pl.Buffered(1)),
        ],
        out_specs=pl.BlockSpec((None, tm, d), lambda b, i: (b, i, 0)),
        out_shape=jax.ShapeDtypeStruct((bsz, seq, d), F32),
        scratch_shapes=[pltpu.VMEM((tm, d), BF16)],
        compiler_params=_cparams(("arbitrary", "arbitrary")),
        name="attn_out_proj",
    )(att_a, att_b, x, mod, w)


def _mlp_in_kernel(x_ref, mod_ref, ln_ref, w_ref, o_ref, h_s):
    @pl.when(pl.program_id(2) == 0)
    def _():
        _norm_modulate(x_ref, ln_ref, mod_ref[4:5, :], mod_ref[3:4, :], h_s)

    y = jnp.maximum(jnp.dot(h_s[...], w_ref[...], preferred_element_type=F32), 0.0)
    o_ref[...] = (y * y).astype(BF16)


def _mlp_in(x, mod, ln, w, tm=1024, tn=1024):
    bsz, seq, d = x.shape
    f = w.shape[1]
    return pl.pallas_call(
        _mlp_in_kernel,
        grid=(bsz, seq // tm, f // tn),
        in_specs=[
            pl.BlockSpec((None, tm, d), lambda b, i, j: (b, i, 0)),
            pl.BlockSpec((None, 6, d), lambda b, i, j: (b, 0, 0)),
            pl.BlockSpec((1, d), lambda b, i, j: (0, 0)),
            pl.BlockSpec((d, tn), lambda b, i, j: (0, j)),
        ],
        out_specs=pl.BlockSpec((None, tm, tn), lambda b, i, j: (b, i, j)),
        out_shape=jax.ShapeDtypeStruct((bsz, seq, f), BF16),
        scratch_shapes=[pltpu.VMEM((tm, d), BF16)],
        compiler_params=_cparams(("arbitrary", "arbitrary", "arbitrary")),
        name="mlp_in",
    )(x, mod, ln, w)


def _mlp_out_kernel(h_ref, x_ref, mod_ref, w_ref, o_ref):
    y = jnp.dot(h_ref[...], w_ref[...], preferred_element_type=F32)
    o_ref[...] = x_ref[...] + mod_ref[5:6, :] * y


def _mlp_out(hid, x, mod, w, tm=512, tn=512):
    bsz, seq, d = x.shape
    f = hid.shape[2]
    return pl.pallas_call(
        _mlp_out_kernel,
        grid=(bsz, seq // tm, d // tn),
        in_specs=[
            pl.BlockSpec((None, tm, f), lambda b, i, j: (b, i, 0)),
            pl.BlockSpec((None, tm, tn), lambda b, i, j: (b, i, j)),
            pl.BlockSpec((None, 6, tn), lambda b, i, j: (b, 0, j)),
            pl.BlockSpec((f, tn), lambda b, i, j: (0, j)),
        ],
        out_specs=pl.BlockSpec((None, tm, tn), lambda b, i, j: (b, i, j)),
        out_shape=jax.ShapeDtypeStruct((bsz, seq, d), F32),
        compiler_params=_cparams(("arbitrary", "arbitrary", "arbitrary")),
        name="mlp_out",
    )(hid, x, mod, w)


def _forward(x, c, ln1, w_ada, b_ada, w_in, q_norm_dil, k_norm_dil, q_norm_na, k_norm_na,
             na_rel_bias, w_out, ln2, w_mlp_in, w_mlp_out, dil_tq):
    bsz, seq, d = x.shape
    depth = w_ada.shape[0]
    rows = seq // GRID_W
    tables = _rope_tables(seq)
    mod_all = _modulation(c, w_ada, b_ada).reshape(depth, bsz, 6, d)
    for layer in range(depth):
        mod = mod_all[layer]
        gains = jnp.stack([q_norm_dil[layer], k_norm_dil[layer],
                           q_norm_na[layer], k_norm_na[layer]]).astype(F32)
        qkv = _qkv_proj(x, mod, ln1[layer][None, :], w_in[layer].astype(BF16), gains, tables)
        att_dil = _dilated_attention(qkv, tq=dil_tq)
        att_na = _neighborhood_attention(qkv, _na_bias_tables(na_rel_bias[layer], rows))
        x = _out_proj(att_dil, att_na, x, mod, w_out[layer].astype(BF16))
        hid = _mlp_in(x, mod, ln2[layer][None, :], w_mlp_in[layer].astype(BF16))
        x = _mlp_out(hid, x, mod, w_mlp_out[layer].astype(BF16))
    return x


def kernel(x, c, ln1, w_ada, b_ada, w_in, q_norm_dil, k_norm_dil, q_norm_na, k_norm_na,
           na_rel_bias, w_out, ln2, w_mlp_in, w_mlp_out):
    return _forward(x, c, ln1, w_ada, b_ada, w_in, q_norm_dil, k_norm_dil, q_norm_na,
                    k_norm_na, na_rel_bias, w_out, ln2, w_mlp_in, w_mlp_out, dil_tq=256)
```

```python
import functools

import numpy as np
import jax
import jax.numpy as jnp
from jax import lax
from jax.experimental import pallas as pl
from jax.experimental.pallas import tpu as pltpu

HEAD_DIM = 128
N_HEADS_DIL = 8
N_HEADS_NA = 8
ROT_DIM = HEAD_DIM // 4
ROPE_THETA = 500000.0
DIL_HALF = 64
DILATIONS = (16, 4, 1)
GRID_W = 64
NA_ROWS = 8
NA_COLS = 16
EPS = 1e-6
NEG_INF = -1e30

V7X_VMEM_BYTES = 64 * 1024 * 1024
VMEM_LIMIT = 56 * 1024 * 1024

F32 = jnp.float32
BF16 = jnp.bfloat16


def _cparams(sem):
    return pltpu.CompilerParams(dimension_semantics=sem, vmem_limit_bytes=VMEM_LIMIT)


def _mod_kernel(ct_ref, w_ref, b_ref, o_ref, act_s, *, nb):
    d = w_ref.shape[0]
    tn = w_ref.shape[1]

    @pl.when((pl.program_id(0) == 0) & (pl.program_id(1) == 0))
    def _():
        ct = ct_ref[...]
        act = ct * (1.0 / (1.0 + jnp.exp(-ct)))
        for b in range(nb):
            act_s[b] = jnp.broadcast_to(act[:, b:b + 1], (d, HEAD_DIM))

    def body(k, accs):
        r0 = pl.multiple_of(k * 8, 8)
        w8 = w_ref[pl.ds(r0, 8), :]
        out = []
        for b, acc in enumerate(accs):
            a = act_s[b, pl.ds(r0, 8), :]
            out.append(acc + w8 * jnp.concatenate([a] * (tn // HEAD_DIM), axis=1))
        return tuple(out)

    accs = lax.fori_loop(0, d // 8, body, tuple(jnp.zeros((8, tn), F32) for _ in range(nb)),
                         unroll=4)
    for b in range(nb):
        o_ref[b:b + 1, :] = jnp.sum(accs[b], axis=0, keepdims=True) + b_ref[...]


def _modulation(c, w_ada, b_ada):
    nl, d, n6 = w_ada.shape
    nb = c.shape[0]
    tn = 1024
    return pl.pallas_call(
        functools.partial(_mod_kernel, nb=nb),
        grid=(nl, n6 // tn),
        in_specs=[
            pl.BlockSpec((d, nb), lambda l, j: (0, 0)),
            pl.BlockSpec((None, d, tn), lambda l, j: (l, 0, j)),
            pl.BlockSpec((None, 1, tn), lambda l, j: (l, 0, j)),
        ],
        out_specs=pl.BlockSpec((None, nb, tn), lambda l, j: (l, 0, j)),
        out_shape=jax.ShapeDtypeStruct((nl, nb, n6), F32),
        scratch_shapes=[pltpu.VMEM((nb, d, HEAD_DIM), F32)],
        compiler_params=_cparams(("arbitrary", "arbitrary")),
        name="adaln_modulation",
    )(c.T, w_ada, b_ada.reshape(nl, 1, n6))


_NORM_CHUNK = 64


def _norm_modulate(x_ref, ln_ref, sc, sh, h_s):
    tm = x_ref.shape[0]
    gain = ln_ref[...] * (1.0 + sc)

    def body(c, carry):
        r0 = pl.multiple_of(c * _NORM_CHUNK, _NORM_CHUNK)
        x = x_ref[pl.ds(r0, _NORM_CHUNK), :]
        ms = jnp.mean(x * x, axis=-1, keepdims=True)
        h_s[pl.ds(r0, _NORM_CHUNK), :] = (x * lax.rsqrt(ms + EPS) * gain + sh).astype(BF16)
        return carry

    lax.fori_loop(0, tm // _NORM_CHUNK, body, 0)


def _qkv_kernel(x_ref, mod_ref, ln_ref, w_ref, gain_ref, cos_ref, sa_ref, sb_ref,
                o_ref, h_s, acc_s):
    scale = HEAD_DIM ** -0.5
    nh = N_HEADS_DIL
    tn = nh * HEAD_DIM
    _norm_modulate(x_ref, ln_ref, mod_ref[1:2, :], mod_ref[0:1, :], h_s)

    for g in range(w_ref.shape[1] // tn):
        acc = acc_s.at[g % 2]
        acc[...] = jnp.dot(h_s[...], w_ref[:, g * tn:(g + 1) * tn], preferred_element_type=F32)
        kind = g % 3
        rope = g < 3
        if kind < 2:
            row = 2 * (g // 3) + kind
            gain = gain_ref[row:row + 1, :] * (scale if kind == 0 else 1.0)
        for h in range(nh):
            a = acc[:, h * HEAD_DIM:(h + 1) * HEAD_DIM]
            if kind == 2:
                o_ref[g * nh + h] = a.astype(BF16)
                continue
            ms = jnp.mean(a * a, axis=-1, keepdims=True)
            y = a * lax.rsqrt(ms + EPS) * gain
            if rope:
                y = (y * cos_ref[...]
                     + pltpu.roll(y, ROT_DIM // 2, 1) * sa_ref[...]
                     + pltpu.roll(y, HEAD_DIM - ROT_DIM // 2, 1) * sb_ref[...])
            o_ref[g * nh + h] = y.astype(BF16)


def _rope_tables(seq):
    half = ROT_DIM // 2
    pos = np.arange(seq, dtype=np.float64)[:, None]
    inv = ROPE_THETA ** (-np.arange(0, ROT_DIM, 2, dtype=np.float64) / ROT_DIM)[None, :]
    ang = pos * inv
    cos = np.ones((seq, HEAD_DIM), np.float64)
    sa = np.zeros((seq, HEAD_DIM), np.float64)
    sb = np.zeros((seq, HEAD_DIM), np.float64)
    cos[:, :half] = np.cos(ang)
    cos[:, half:ROT_DIM] = np.cos(ang)
    sa[:, half:ROT_DIM] = np.sin(ang)
    sb[:, :half] = -np.sin(ang)
    return (jnp.asarray(cos, F32), jnp.asarray(sa, F32), jnp.asarray(sb, F32))


def _qkv_proj(x, mod, ln, w, gains, tables, tm=256):
    bsz, seq, d = x.shape
    n = w.shape[1]
    tn = N_HEADS_DIL * HEAD_DIM
    cos, sa, sb = tables
    tab_spec = pl.BlockSpec((tm, HEAD_DIM), lambda b, i: (i, 0))
    return pl.pallas_call(
        _qkv_kernel,
        grid=(bsz, seq // tm),
        in_specs=[
            pl.BlockSpec((None, tm, d), lambda b, i: (b, i, 0)),
            pl.BlockSpec((None, 6, d), lambda b, i: (b, 0, 0)),
            pl.BlockSpec((1, d), lambda b, i: (0, 0)),
            pl.BlockSpec((d, n), lambda b, i: (0, 0), pipeline_mode=pl.Buffered(1)),
            pl.BlockSpec((4, HEAD_DIM), lambda b, i: (0, 0)),
            tab_spec, tab_spec, tab_spec,
        ],
        out_specs=pl.BlockSpec((None, n // HEAD_DIM, tm, HEAD_DIM), lambda b, i: (b, 0, i, 0)),
        out_shape=jax.ShapeDtypeStruct((bsz, n // HEAD_DIM, seq, HEAD_DIM), BF16),
        scratch_shapes=[pltpu.VMEM((tm, d), BF16), pltpu.VMEM((2, tm, tn), F32)],
        compiler_params=_cparams(("arbitrary", "arbitrary")),
        name="qkv_proj",
    )(x, mod, ln, w, gains, cos, sa, sb)


_CAST_CHUNK = 256


def _dil_mask_tables(tq):
    kw = tq + 2 * DIL_HALF
    i = np.arange(tq)[:, None]
    j = np.arange(kw)[None, :]
    tabs = [np.where(np.abs(j - i - DIL_HALF * case) <= DIL_HALF, 0.0, NEG_INF) for case in range(3)]
    return jnp.asarray(np.stack(tabs), F32)


def _dil_kernel(q_ref, k_ref, v_ref, mask_ref, o_ref, q32, k32, v32, acc_s, m_s, l_s,
                s0, s1, p0, p1, vb0, vb1, mb0, mb1, lb0, lb1, *, tq):
    seq = q_ref.shape[0]
    kw = tq + 2 * DIL_HALF
    s_buf, p_buf, v_buf, m_buf, l_buf = (s0, s1), (p0, p1), (vb0, vb1), (mb0, mb1), (lb0, lb1)

    def cast_body(c, carry):
        r0 = pl.multiple_of(c * _CAST_CHUNK, _CAST_CHUNK)
        rows = pl.ds(r0, _CAST_CHUNK)
        q32[rows, :] = q_ref[rows, :].astype(F32)
        k32[rows, :] = k_ref[rows, :].astype(F32)
        v32[rows, :] = v_ref[rows, :].astype(F32)
        return carry

    lax.fori_loop(0, seq // _CAST_CHUNK, cast_body, 0)

    def run_pattern(dil, mode):
        cls_len = seq // dil
        nblk = cls_len // tq
        total = dil * nblk

        def geom(it):
            it = jnp.minimum(it, total - 1)
            r = it // nblk
            n0 = (it % nblk) * tq
            nk0 = jnp.clip(n0 - DIL_HALF, 0, cls_len - kw)
            if dil == 1:
                qrows = pl.ds(pl.multiple_of(n0, tq), tq)
                krows = pl.ds(pl.multiple_of(nk0, 8), kw)
            else:
                qrows = pl.ds(r + dil * n0, tq, stride=dil)
                krows = pl.ds(r + dil * nk0, kw, stride=dil)
            return qrows, krows, (n0 - nk0) // DIL_HALF

        def scores(it, slot):
            qrows, krows, case = geom(it)
            q = q32[qrows, :].astype(BF16)
            k = k32[krows, :].astype(BF16)
            v_buf[slot][...] = v32[krows, :].astype(BF16)
            s = lax.dot_general(q, k, (((1,), (1,)), ((), ())), preferred_element_type=F32)
            s_buf[slot][...] = s + mask_ref[case]

        def softmax(slot):
            s = s_buf[slot][...]
            m = jnp.max(s, axis=-1, keepdims=True)
            p = jnp.exp(s - m)
            l = jnp.sum(p, axis=-1, keepdims=True)
            p_buf[slot][...] = p.astype(BF16)
            m_buf[slot][...] = jnp.broadcast_to(m, (tq, HEAD_DIM))
            l_buf[slot][...] = jnp.broadcast_to(l, (tq, HEAD_DIM))

        def values(it, slot):
            qrows, _, _ = geom(it)
            o = jnp.dot(p_buf[slot][...], v_buf[slot][...], preferred_element_type=F32)
            mb = m_buf[slot][...]
            lb = l_buf[slot][...]
            if mode == "init":
                acc_s[qrows, :] = o
                m_s[qrows, :] = mb
                l_s[qrows, :] = lb
            else:
                m_old = m_s[qrows, :]
                m_new = jnp.maximum(m_old, mb)
                a_old = jnp.exp(m_old - m_new)
                a_new = jnp.exp(mb - m_new)
                acc = acc_s[qrows, :] * a_old + o * a_new
                l_new = l_s[qrows, :] * a_old + lb * a_new
                if mode == "merge":
                    acc_s[qrows, :] = acc
                    m_s[qrows, :] = m_new
                    l_s[qrows, :] = l_new
                else:
                    o_ref[qrows, :] = (acc / l_new).astype(BF16)

        scores(0, 0)
        scores(1, 1)
        softmax(0)

        def body(j, carry):
            w = 2 * j
            values(w, 0)
            scores(w + 2, 0)
            softmax(1)
            values(w + 1, 1)
            scores(w + 3, 1)
            softmax(0)
            return carry

        lax.fori_loop(0, total // 2, body, 0)

    run_pattern(DILATIONS[0], "init")
    run_pattern(DILATIONS[1], "merge")
    run_pattern(DILATIONS[2], "final")


def _dilated_attention(qkv, tq=256):
    bsz, _, seq, hd = qkv.shape
    nh = N_HEADS_DIL
    kw = tq + 2 * DIL_HALF
    masks = _dil_mask_tables(tq)

    def spec(off):
        return pl.BlockSpec((None, None, seq, hd), lambda b, h: (b, off + h, 0, 0))

    def two(shape, dtype):
        return [pltpu.VMEM(shape, dtype), pltpu.VMEM(shape, dtype)]

    return pl.pallas_call(
        functools.partial(_dil_kernel, tq=tq),
        grid=(bsz, nh),
        in_specs=[spec(0), spec(nh), spec(2 * nh),
                  pl.BlockSpec(masks.shape, lambda b, h: (0, 0, 0))],
        out_specs=pl.BlockSpec((None, None, seq, hd), lambda b, h: (b, h, 0, 0)),
        out_shape=jax.ShapeDtypeStruct((bsz, nh, seq, hd), BF16),
        scratch_shapes=([pltpu.VMEM((seq, hd), F32) for _ in range(6)]
                        + two((tq, kw), F32) + two((tq, kw), BF16) + two((kw, hd), BF16)
                        + two((tq, hd), F32) + two((tq, hd), F32)),
        compiler_params=_cparams(("arbitrary", "arbitrary")),
        name="dilated_attention",
    )(qkv, qkv, qkv, masks)


_NA_QROWS = 4
_NA_KROWS = NA_ROWS + _NA_QROWS - 1
_NA_CHUNK = 64


def _na_key_start(i0, rows):
    return np.clip(i0 - NA_ROWS // 2, 0, rows - _NA_KROWS)


def _na_offsets(rows):
    return tuple(sorted({int(i0 - _na_key_start(i0, rows)) for i0 in range(0, rows, _NA_QROWS)}))


def _na_bias_tables(rpb, rows):
    nq = _NA_QROWS * GRID_W
    nk = _NA_KROWS * GRID_W
    nh, nrow, ncol = rpb.shape
    w = GRID_W
    u = jnp.pad(rpb.astype(F32), ((0, 0), (0, 0), (w - NA_COLS, w + NA_COLS - ncol)))
    toep = jnp.broadcast_to(u[:, :, None, :], (nh, nrow, w, 2 * w)).reshape(nh, nrow, 2 * w * w)
    toep = toep[:, :, :w * (2 * w - 1)].reshape(nh, nrow, w, 2 * w - 1)[:, :, :, w - 1:]
    cs = np.clip(np.arange(w) - NA_COLS // 2, 0, w - NA_COLS)
    cvalid = (np.arange(w)[None, :] >= cs[:, None]) & (np.arange(w)[None, :] < cs[:, None] + NA_COLS)
    def row_pattern(i0):
        ks = int(_na_key_start(i0, rows))
        roff, rvalid = [], []
        for qi in range(i0, i0 + _NA_QROWS):
            rs = int(np.clip(qi - NA_ROWS // 2, 0, rows - NA_ROWS))
            for kr in range(ks, ks + _NA_KROWS):
                roff.append(int(np.clip(kr - qi + NA_ROWS - 1, 0, nrow - 1)))
                rvalid.append(rs <= kr < rs + NA_ROWS)
        return roff, rvalid

    slabs, rvalids = [], []
    for off in _na_offsets(rows):
        blocks = [i0 for i0 in range(0, rows, _NA_QROWS) if i0 - _na_key_start(i0, rows) == off]
        roff, rvalid = row_pattern(blocks[0])
        assert all(row_pattern(i0) == (roff, rvalid) for i0 in blocks)
        slabs += [toep[:, r] for r in roff]
        rvalids += rvalid
    ncase = len(_na_offsets(rows))
    tab = jnp.stack(slabs, axis=1).reshape(nh, ncase, _NA_QROWS, _NA_KROWS, w, w)
    rvalid = np.asarray(rvalids).reshape(ncase, _NA_QROWS, _NA_KROWS)
    valid = rvalid[:, :, :, None, None] & cvalid[None, None, None]
    tab = jnp.where(valid[None], tab, NEG_INF)
    return tab.transpose(0, 1, 3, 5, 2, 4).reshape(nh, ncase, nk, nq)


def _na_kernel(q_ref, k_ref, v_ref, bias_ref, o_ref, s0, s1, p0, p1, l0, l1, *, offsets):
    seq = q_ref.shape[0]
    rows = seq // GRID_W
    nq = _NA_QROWS * GRID_W
    nk = _NA_KROWS * GRID_W
    nblk = rows // _NA_QROWS
    s_buf, p_buf, l_buf = (s0, s1), (p0, p1), (l0, l1)

    def geom(blk):
        blk = jnp.minimum(blk, nblk - 1)
        i0 = blk * _NA_QROWS
        ks = jnp.clip(i0 - NA_ROWS // 2, 0, rows - _NA_KROWS)
        case = sum((i0 - ks >= off).astype(jnp.int32) for off in offsets[1:])
        qrows = pl.ds(pl.multiple_of(i0 * GRID_W, nq), nq)
        krows = pl.ds(pl.multiple_of(ks * GRID_W, GRID_W), nk)
        return qrows, krows, case

    def scores(blk, slot):
        qrows, krows, case = geom(blk)
        st = lax.dot_general(k_ref[krows, :], q_ref[qrows, :], (((1,), (1,)), ((), ())),
                             preferred_element_type=F32)
        s_buf[slot][...] = st + bias_ref[case]

    def softmax(slot):
        s = s_buf[slot]
        nchunk = nk // _NA_CHUNK
        m = s[0:_NA_CHUNK, :]
        for c in range(1, nchunk):
            m = jnp.maximum(m, s[c * _NA_CHUNK:(c + 1) * _NA_CHUNK, :])
        m = jnp.max(m, axis=0, keepdims=True)
        l = jnp.zeros((_NA_CHUNK, nq), F32)
        for c in range(nchunk):
            p = jnp.exp(s[c * _NA_CHUNK:(c + 1) * _NA_CHUNK, :] - m)
            l = l + p
            p_buf[slot][c * _NA_CHUNK:(c + 1) * _NA_CHUNK, :] = p.astype(BF16)
        l_buf[slot][...] = jnp.broadcast_to(jnp.sum(l, axis=0, keepdims=True), l_buf[slot].shape)

    def values(blk, slot):
        qrows, krows, _ = geom(blk)
        ot = lax.dot_general(v_ref[krows, :], p_buf[slot][...], (((0,), (0,)), ((), ())),
                             preferred_element_type=F32)
        ot = ot * (1.0 / l_buf[slot][0:1, :])
        o_ref[qrows, :] = ot.T.astype(BF16)

    scores(0, 0)
    scores(1, 1)
    softmax(0)

    def body(j, carry):
        w = 2 * j
        scores(w + 2, 0)
        softmax(1)
        values(w, 0)
        scores(w + 3, 1)
        softmax(0)
        values(w + 1, 1)
        return carry

    lax.fori_loop(0, nblk // 2, body, 0)


def _neighborhood_attention(qkv, bias, offsets):
    bsz, _, seq, hd = qkv.shape
    nh = N_HEADS_NA
    base = 3 * N_HEADS_DIL
    ncase, nk, nq = bias.shape[1:]

    def spec(off):
        return pl.BlockSpec((None, None, seq, hd), lambda b, h: (b, base + off + h, 0, 0))

    return pl.pallas_call(
        functools.partial(_na_kernel, offsets=offsets),
        grid=(bsz, nh),
        in_specs=[spec(0), spec(nh), spec(2 * nh),
                  pl.BlockSpec((None, ncase, nk, nq), lambda b, h: (h, 0, 0, 0))],
        out_specs=pl.BlockSpec((None, None, seq, hd), lambda b, h: (b, h, 0, 0)),
        out_shape=jax.ShapeDtypeStruct((bsz, nh, seq, hd), BF16),
        scratch_shapes=[pltpu.VMEM((nk, nq), F32), pltpu.VMEM((nk, nq), F32),
                        pltpu.VMEM((nk, nq), BF16), pltpu.VMEM((nk, nq), BF16),
                        pltpu.VMEM((8, nq), F32), pltpu.VMEM((8, nq), F32)],
        compiler_params=_cparams(("arbitrary", "arbitrary")),
        name="neighborhood_attention",
    )(qkv, qkv, qkv, bias)


def _out_kernel(a_ref, b_ref, x_ref, mod_ref, w_ref, o_ref, mix_s):
    na = a_ref.shape[0]
    for h in range(na):
        mix_s[:, h * HEAD_DIM:(h + 1) * HEAD_DIM] = a_ref[h]
    for h in range(b_ref.shape[0]):
        mix_s[:, (na + h) * HEAD_DIM:(na + h + 1) * HEAD_DIM] = b_ref[h]
    y = jnp.dot(mix_s[...], w_ref[...], preferred_element_type=F32)
    o_ref[...] = x_ref[...] + mod_ref[2:3, :] * y


def _out_proj(att_a, att_b, x, mod, w, tm=512):
    bsz, seq, d = x.shape
    na, nb = att_a.shape[1], att_b.shape[1]
    return pl.pallas_call(
        _out_kernel,
        grid=(bsz, seq // tm),
        in_specs=[
            pl.BlockSpec((None, na, tm, HEAD_DIM), lambda b, i: (b, 0, i, 0)),
            pl.BlockSpec((None, nb, tm, HEAD_DIM), lambda b, i: (b, 0, i, 0)),
            pl.BlockSpec((None, tm, d), lambda b, i: (b, i, 0)),
            pl.BlockSpec((None, 6, d), lambda b, i: (b, 0, 0)),
            pl.BlockSpec((d, d), lambda b, i: (0, 0), pipeline_mode=pl.Buffered(1)),
        ],
        out_specs=pl.BlockSpec((None, tm, d), lambda b, i: (b, i, 0)),
        out_shape=jax.ShapeDtypeStruct((bsz, seq, d), F32),
        scratch_shapes=[pltpu.VMEM((tm, d), BF16)],
        compiler_params=_cparams(("arbitrary", "arbitrary")),
        name="attn_out_proj",
    )(att_a, att_b, x, mod, w)


def _mlp_in_kernel(x_ref, mod_ref, ln_ref, w_ref, o_ref, h_s):
    @pl.when(pl.program_id(2) == 0)
    def _():
        _norm_modulate(x_ref, ln_ref, mod_ref[4:5, :], mod_ref[3:4, :], h_s)

    y = jnp.maximum(jnp.dot(h_s[...], w_ref[...], preferred_element_type=F32), 0.0)
    o_ref[...] = (y * y).astype(BF16)


def _mlp_in(x, mod, ln, w, tm=1024, tn=1024):
    bsz, seq, d = x.shape
    f = w.shape[1]
    return pl.pallas_call(
        _mlp_in_kernel,
        grid=(bsz, seq // tm, f // tn),
        in_specs=[
            pl.BlockSpec((None, tm, d), lambda b, i, j: (b, i, 0)),
            pl.BlockSpec((None, 6, d), lambda b, i, j: (b, 0, 0)),
            pl.BlockSpec((1, d), lambda b, i, j: (0, 0)),
            pl.BlockSpec((d, tn), lambda b, i, j: (0, j)),
        ],
        out_specs=pl.BlockSpec((None, tm, tn), lambda b, i, j: (b, i, j)),
        out_shape=jax.ShapeDtypeStruct((bsz, seq, f), BF16),
        scratch_shapes=[pltpu.VMEM((tm, d), BF16)],
        compiler_params=_cparams(("arbitrary", "arbitrary", "arbitrary")),
        name="mlp_in",
    )(x, mod, ln, w)


def _mlp_out_kernel(h_ref, x_ref, mod_ref, w_ref, o_ref):
    y = jnp.dot(h_ref[...], w_ref[...], preferred_element_type=F32)
    o_ref[...] = x_ref[...] + mod_ref[5:6, :] * y


def _mlp_out(hid, x, mod, w, tm=512, tn=512):
    bsz, seq, d = x.shape
    f = hid.shape[2]
    return pl.pallas_call(
        _mlp_out_kernel,
        grid=(bsz, seq // tm, d // tn),
        in_specs=[
            pl.BlockSpec((None, tm, f), lambda b, i, j: (b, i, 0)),
            pl.BlockSpec((None, tm, tn), lambda b, i, j: (b, i, j)),
            pl.BlockSpec((None, 6, tn), lambda b, i, j: (b, 0, j)),
            pl.BlockSpec((f, tn), lambda b, i, j: (0, j)),
        ],
        out_specs=pl.BlockSpec((None, tm, tn), lambda b, i, j: (b, i, j)),
        out_shape=jax.ShapeDtypeStruct((bsz, seq, d), F32),
        compiler_params=_cparams(("arbitrary", "arbitrary", "arbitrary")),
        name="mlp_out",
    )(hid, x, mod, w)


def _forward(x, c, ln1, w_ada, b_ada, w_in, q_norm_dil, k_norm_dil, q_norm_na, k_norm_na,
             na_rel_bias, w_out, ln2, w_mlp_in, w_mlp_out, dil_tq):
    bsz, seq, d = x.shape
    depth = w_ada.shape[0]
    rows = seq // GRID_W
    tables = _rope_tables(seq)
    mod_all = _modulation(c, w_ada, b_ada).reshape(depth, bsz, 6, d)
    for layer in range(depth):
        mod = mod_all[layer]
        gains = jnp.stack([q_norm_dil[layer], k_norm_dil[layer],
                           q_norm_na[layer], k_norm_na[layer]]).astype(F32)
        qkv = _qkv_proj(x, mod, ln1[layer][None, :], w_in[layer].astype(BF16), gains, tables)
        att_dil = _dilated_attention(qkv, tq=dil_tq)
        att_na = _neighborhood_attention(qkv, _na_bias_tables(na_rel_bias[layer], rows),
                                         _na_offsets(rows))
        x = _out_proj(att_dil, att_na, x, mod, w_out[layer].astype(BF16))
        hid = _mlp_in(x, mod, ln2[layer][None, :], w_mlp_in[layer].astype(BF16))
        x = _mlp_out(hid, x, mod, w_mlp_out[layer].astype(BF16))
    return x


def kernel(x, c, ln1, w_ada, b_ada, w_in, q_norm_dil, k_norm_dil, q_norm_na, k_norm_na,
           na_rel_bias, w_out, ln2, w_mlp_in, w_mlp_out):
    return _forward(x, c, ln1, w_ada, b_ada, w_in, q_norm_dil, k_norm_dil, q_norm_na,
                    k_norm_na, na_rel_bias, w_out, ln2, w_mlp_in, w_mlp_out, dil_tq=256)
```

```python
import functools

import numpy as np
import jax
import jax.numpy as jnp
from jax import lax
from jax.experimental import pallas as pl
from jax.experimental.pallas import tpu as pltpu

HEAD_DIM = 128
N_HEADS_DIL = 8
N_HEADS_NA = 8
ROT_DIM = HEAD_DIM // 4
ROPE_THETA = 500000.0
DIL_HALF = 64
DILATIONS = (16, 4, 1)
GRID_W = 64
NA_ROWS = 8
NA_COLS = 16
EPS = 1e-6
NEG_INF = -1e30
LOG2E = float(np.log2(np.e))

V7X_VMEM_BYTES = 64 * 1024 * 1024
VMEM_LIMIT = 56 * 1024 * 1024
DIL_VMEM_LIMIT = 60 * 1024 * 1024

F32 = jnp.float32
BF16 = jnp.bfloat16


def _cparams(sem):
    return pltpu.CompilerParams(dimension_semantics=sem, vmem_limit_bytes=VMEM_LIMIT)


def _mod_kernel(ct_ref, w_ref, b_ref, o_ref, act_s, *, nb):
    d = w_ref.shape[0]
    tn = w_ref.shape[1]

    @pl.when((pl.program_id(0) == 0) & (pl.program_id(1) == 0))
    def _():
        ct = ct_ref[...]
        act = ct * (1.0 / (1.0 + jnp.exp(-ct)))
        for b in range(nb):
            act_s[b] = jnp.broadcast_to(act[:, b:b + 1], (d, HEAD_DIM))

    def body(k, accs):
        r0 = pl.multiple_of(k * 8, 8)
        w8 = w_ref[pl.ds(r0, 8), :]
        out = []
        for b, acc in enumerate(accs):
            a = act_s[b, pl.ds(r0, 8), :]
            out.append(acc + w8 * jnp.concatenate([a] * (tn // HEAD_DIM), axis=1))
        return tuple(out)

    accs = lax.fori_loop(0, d // 8, body, tuple(jnp.zeros((8, tn), F32) for _ in range(nb)),
                         unroll=4)
    for b in range(nb):
        o_ref[b:b + 1, :] = jnp.sum(accs[b], axis=0, keepdims=True) + b_ref[...]


def _modulation(c, w_ada, b_ada):
    nl, d, n6 = w_ada.shape
    nb = c.shape[0]
    tn = 1024
    return pl.pallas_call(
        functools.partial(_mod_kernel, nb=nb),
        grid=(nl, n6 // tn),
        in_specs=[
            pl.BlockSpec((d, nb), lambda l, j: (0, 0)),
            pl.BlockSpec((None, d, tn), lambda l, j: (l, 0, j)),
            pl.BlockSpec((None, 1, tn), lambda l, j: (l, 0, j)),
        ],
        out_specs=pl.BlockSpec((None, nb, tn), lambda l, j: (l, 0, j)),
        out_shape=jax.ShapeDtypeStruct((nl, nb, n6), F32),
        scratch_shapes=[pltpu.VMEM((nb, d, HEAD_DIM), F32)],
        compiler_params=_cparams(("arbitrary", "arbitrary")),
        name="adaln_modulation",
    )(c.T, w_ada, b_ada.reshape(nl, 1, n6))


_NORM_CHUNK = 64


def _norm_modulate(x_ref, ln_ref, sc, sh, h_s):
    tm = x_ref.shape[0]
    gain = ln_ref[...] * (1.0 + sc)

    def body(c, carry):
        r0 = pl.multiple_of(c * _NORM_CHUNK, _NORM_CHUNK)
        x = x_ref[pl.ds(r0, _NORM_CHUNK), :]
        ms = jnp.mean(x * x, axis=-1, keepdims=True)
        h_s[pl.ds(r0, _NORM_CHUNK), :] = (x * lax.rsqrt(ms + EPS) * gain + sh).astype(BF16)
        return carry

    lax.fori_loop(0, tm // _NORM_CHUNK, body, 0)


def _qkv_kernel(xc_ref, xn_ref, mod_ref, ln_ref, w_ref, gain_ref, cos_ref, sa_ref, sb_ref,
                oc_ref, on_ref, hc_s, hn_s, acc_s):
    scale = HEAD_DIM ** -0.5 * LOG2E
    nh = N_HEADS_DIL
    tn = nh * HEAD_DIM
    _norm_modulate(xc_ref, ln_ref, mod_ref[1:2, :], mod_ref[0:1, :], hc_s)
    _norm_modulate(xn_ref, ln_ref, mod_ref[1:2, :], mod_ref[0:1, :], hn_s)

    for g in range(w_ref.shape[1] // tn):
        rope = g < 3
        h_s, o_ref = (hc_s, oc_ref) if rope else (hn_s, on_ref)
        acc = acc_s.at[g % 2]
        acc[...] = jnp.dot(h_s[...], w_ref[:, g * tn:(g + 1) * tn], preferred_element_type=F32)
        kind = g % 3
        if kind < 2:
            row = 2 * (g // 3) + kind
            gain = gain_ref[row:row + 1, :] * (scale if kind == 0 else 1.0)
        for h in range(nh):
            a = acc[:, h * HEAD_DIM:(h + 1) * HEAD_DIM]
            if kind == 2:
                o_ref[kind * nh + h] = a.astype(BF16)
                continue
            ms = jnp.mean(a * a, axis=-1, keepdims=True)
            y = a * lax.rsqrt(ms + EPS) * gain
            if rope:
                y = (y * cos_ref[...]
                     + pltpu.roll(y, ROT_DIM // 2, 1) * sa_ref[...]
                     + pltpu.roll(y, HEAD_DIM - ROT_DIM // 2, 1) * sb_ref[...])
            o_ref[kind * nh + h] = y.astype(BF16)


def _rope_tables(seq):
    half = ROT_DIM // 2
    dmax = DILATIONS[0]
    pos = np.arange(seq, dtype=np.float64).reshape(seq // dmax, dmax).T.reshape(seq, 1)
    inv = ROPE_THETA ** (-np.arange(0, ROT_DIM, 2, dtype=np.float64) / ROT_DIM)[None, :]
    ang = pos * inv
    cos = np.ones((seq, HEAD_DIM), np.float64)
    sa = np.zeros((seq, HEAD_DIM), np.float64)
    sb = np.zeros((seq, HEAD_DIM), np.float64)
    cos[:, :half] = np.cos(ang)
    cos[:, half:ROT_DIM] = np.cos(ang)
    sa[:, half:ROT_DIM] = np.sin(ang)
    sb[:, :half] = -np.sin(ang)
    return (jnp.asarray(cos, F32), jnp.asarray(sa, F32), jnp.asarray(sb, F32))


def _qkv_proj(x, mod, ln, w, gains, tables, tm=256):
    bsz, seq, d = x.shape
    n = w.shape[1]
    tn = N_HEADS_DIL * HEAD_DIM
    nslab = n // HEAD_DIM // 2
    dmax = DILATIONS[0]
    cls_tiles = seq // dmax // tm
    cos, sa, sb = tables
    x_cls = x.reshape(bsz, seq // dmax, dmax * d)
    tab_spec = pl.BlockSpec((tm, HEAD_DIM), lambda b, t: (t, 0))
    out_spec = pl.BlockSpec((None, nslab, tm, HEAD_DIM), lambda b, t: (b, 0, t, 0))
    out_shape = jax.ShapeDtypeStruct((bsz, nslab, seq, HEAD_DIM), BF16)
    return pl.pallas_call(
        _qkv_kernel,
        grid=(bsz, seq // tm),
        in_specs=[
            pl.BlockSpec((None, tm, d), lambda b, t: (b, t % cls_tiles, t // cls_tiles)),
            pl.BlockSpec((None, tm, d), lambda b, t: (b, t, 0)),
            pl.BlockSpec((None, 6, d), lambda b, t: (b, 0, 0)),
            pl.BlockSpec((1, d), lambda b, t: (0, 0)),
            pl.BlockSpec((d, n), lambda b, t: (0, 0), pipeline_mode=pl.Buffered(1)),
            pl.BlockSpec((4, HEAD_DIM), lambda b, t: (0, 0)),
            tab_spec, tab_spec, tab_spec,
        ],
        out_specs=[out_spec, out_spec],
        out_shape=[out_shape, out_shape],
        scratch_shapes=[pltpu.VMEM((tm, d), BF16), pltpu.VMEM((tm, d), BF16),
                        pltpu.VMEM((2, tm, tn), F32)],
        compiler_params=_cparams(("arbitrary", "arbitrary")),
        name="qkv_proj",
    )(x_cls, x, mod, ln, w, gains, cos, sa, sb)


_CAST_CHUNK = 256


def _dil_mask_tables(tq):
    kw = tq + 2 * DIL_HALF
    i = np.arange(tq)[:, None]
    j = np.arange(kw)[None, :]
    tabs = [np.where(np.abs(j - i - DIL_HALF * case) <= DIL_HALF, 0.0, NEG_INF) for case in range(3)]
    return jnp.asarray(np.stack(tabs), F32)


def _dil_kernel(q_ref, k_ref, v_ref, mask_ref, o_ref, a0, a1, a2, b0, b1, b2, c0, c1, c2,
                s0, s1, p0, p1, vb0, vb1, mb0, mb1, lb0, lb1, *, tq):
    seq = q_ref.shape[0]
    kw = tq + 2 * DIL_HALF
    s_buf, p_buf, v_buf, m_buf, l_buf = (s0, s1), (p0, p1), (vb0, vb1), (mb0, mb1), (lb0, lb1)
    set_a, set_b, set_c = (a0, a1, a2), (b0, b1, b2), (c0, c1, c2)
    step = DILATIONS[0] // DILATIONS[1]

    def finer_rows(dil, r, n0, size):
        fine = dil // step
        return pl.ds((r % fine) * (seq // fine) + r // fine + step * n0, size, stride=step)

    def reorder(dil, src, dst):
        cls_len = seq // dil
        nchunk = cls_len // _CAST_CHUNK

        def body(it, carry):
            r = it // nchunk
            n0 = (it % nchunk) * _CAST_CHUNK
            rows = pl.ds(pl.multiple_of(r * cls_len + n0, _CAST_CHUNK), _CAST_CHUNK)
            for s_ref, d_ref in zip(src, dst):
                d_ref[finer_rows(dil, r, n0, _CAST_CHUNK), :] = s_ref[rows, :].astype(F32)
            return carry

        lax.fori_loop(0, dil * nchunk, body, 0)

    def run_pattern(dil, mode, src, acc_in, acc_out):
        cls_len = seq // dil
        nblk = cls_len // tq
        total = dil * nblk
        qs, ks, vs = src

        def geom(it):
            it = jnp.minimum(it, total - 1)
            r = it // nblk
            n0 = (it % nblk) * tq
            nk0 = jnp.clip(n0 - DIL_HALF, 0, cls_len - kw)
            qrows = pl.ds(pl.multiple_of(r * cls_len + n0, tq), tq)
            krows = pl.ds(pl.multiple_of(r * cls_len + nk0, DIL_HALF), kw)
            return qrows, krows, (n0 - nk0) // DIL_HALF, r, n0

        def scores(it, slot):
            qrows, krows, case, _, _ = geom(it)
            q = qs[qrows, :].astype(BF16)
            k = ks[krows, :].astype(BF16)
            v_buf[slot][...] = vs[krows, :].astype(BF16)
            s = lax.dot_general(q, k, (((1,), (1,)), ((), ())), preferred_element_type=F32)
            s_buf[slot][...] = s + mask_ref[case]

        def softmax(slot):
            s = s_buf[slot][...]
            m = jnp.max(s, axis=-1, keepdims=True)
            p = jnp.exp2(s - m)
            l = jnp.sum(p, axis=-1, keepdims=True)
            p_buf[slot][...] = p.astype(BF16)
            m_buf[slot][...] = jnp.broadcast_to(m, (tq, HEAD_DIM))
            l_buf[slot][...] = jnp.broadcast_to(l, (tq, HEAD_DIM))

        def values(it, slot):
            qrows, _, _, r, n0 = geom(it)
            o = jnp.dot(p_buf[slot][...], v_buf[slot][...], preferred_element_type=F32)
            mb = m_buf[slot][...]
            lb = l_buf[slot][...]
            if mode != "init":
                acc_i, m_i, l_i = acc_in
                m_old = m_i[qrows, :]
                m_new = jnp.maximum(m_old, mb)
                a_old = jnp.exp2(m_old - m_new)
                a_new = jnp.exp2(mb - m_new)
                o = acc_i[qrows, :] * a_old + o * a_new
                lb = l_i[qrows, :] * a_old + lb * a_new
                mb = m_new
            if mode == "final":
                o_ref[qrows, :] = (o / lb).astype(BF16)
            else:
                acc_o, m_o, l_o = acc_out
                orows = finer_rows(dil, r, n0, tq)
                acc_o[orows, :] = o
                m_o[orows, :] = mb
                l_o[orows, :] = lb

        scores(0, 0)
        scores(1, 1)
        softmax(0)

        def body(j, carry):
            w = 2 * j
            values(w, 0)
            scores(w + 2, 0)
            softmax(1)
            values(w + 1, 1)
            scores(w + 3, 1)
            softmax(0)
            return carry

        lax.fori_loop(0, total // 2, body, 0)

    d16, d4, d1 = DILATIONS
    run_pattern(d16, "init", (q_ref, k_ref, v_ref), None, set_a)
    reorder(d16, (q_ref, k_ref, v_ref), set_b)
    run_pattern(d4, "merge", set_b, set_a, set_c)
    reorder(d4, set_b, set_a)
    run_pattern(d1, "final", set_a, set_c, None)


def _dilated_attention(qkv, tq=256):
    bsz, _, seq, hd = qkv.shape
    nh = N_HEADS_DIL
    kw = tq + 2 * DIL_HALF
    masks = _dil_mask_tables(tq)

    def spec(off):
        return pl.BlockSpec((None, None, seq, hd), lambda b, h: (b, off + h, 0, 0))

    def two(shape, dtype):
        return [pltpu.VMEM(shape, dtype), pltpu.VMEM(shape, dtype)]

    return pl.pallas_call(
        functools.partial(_dil_kernel, tq=tq),
        grid=(bsz, nh),
        in_specs=[spec(0), spec(nh), spec(2 * nh),
                  pl.BlockSpec(masks.shape, lambda b, h: (0, 0, 0), pipeline_mode=pl.Buffered(1))],
        out_specs=pl.BlockSpec((None, None, seq, hd), lambda b, h: (b, h, 0, 0)),
        out_shape=jax.ShapeDtypeStruct((bsz, nh, seq, hd), BF16),
        scratch_shapes=([pltpu.VMEM((seq, hd), F32) for _ in range(9)]
                        + two((tq, kw), F32) + two((tq, kw), BF16) + two((kw, hd), BF16)
                        + two((tq, hd), F32) + two((tq, hd), F32)),
        compiler_params=pltpu.CompilerParams(dimension_semantics=("arbitrary", "arbitrary"),
                                             vmem_limit_bytes=DIL_VMEM_LIMIT),
        name="dilated_attention",
    )(qkv, qkv, qkv, masks)


_NA_QROWS = 4
_NA_KROWS = NA_ROWS + _NA_QROWS - 1
_NA_CHUNK = 64


def _na_key_start(i0, rows):
    return np.clip(i0 - NA_ROWS // 2, 0, rows - _NA_KROWS)


def _na_offsets(rows):
    return tuple(sorted({int(i0 - _na_key_start(i0, rows)) for i0 in range(0, rows, _NA_QROWS)}))


def _na_bias_tables(rpb, rows):
    nq = _NA_QROWS * GRID_W
    nk = _NA_KROWS * GRID_W
    nh, nrow, ncol = rpb.shape
    w = GRID_W
    u = jnp.pad(rpb.astype(F32), ((0, 0), (0, 0), (w - NA_COLS, w + NA_COLS - ncol)))
    toep = jnp.broadcast_to(u[:, :, None, :], (nh, nrow, w, 2 * w)).reshape(nh, nrow, 2 * w * w)
    toep = toep[:, :, :w * (2 * w - 1)].reshape(nh, nrow, w, 2 * w - 1)[:, :, :, w - 1:]
    cs = np.clip(np.arange(w) - NA_COLS // 2, 0, w - NA_COLS)
    cvalid = (np.arange(w)[None, :] >= cs[:, None]) & (np.arange(w)[None, :] < cs[:, None] + NA_COLS)
    def row_pattern(i0):
        ks = int(_na_key_start(i0, rows))
        roff, rvalid = [], []
        for qi in range(i0, i0 + _NA_QROWS):
            rs = int(np.clip(qi - NA_ROWS // 2, 0, rows - NA_ROWS))
            for kr in range(ks, ks + _NA_KROWS):
                roff.append(int(np.clip(kr - qi + NA_ROWS - 1, 0, nrow - 1)))
                rvalid.append(rs <= kr < rs + NA_ROWS)
        return roff, rvalid

    slabs, rvalids = [], []
    for off in _na_offsets(rows):
        blocks = [i0 for i0 in range(0, rows, _NA_QROWS) if i0 - _na_key_start(i0, rows) == off]
        roff, rvalid = row_pattern(blocks[0])
        assert all(row_pattern(i0) == (roff, rvalid) for i0 in blocks)
        slabs += [toep[:, r] for r in roff]
        rvalids += rvalid
    ncase = len(_na_offsets(rows))
    tab = jnp.stack(slabs, axis=1).reshape(nh, ncase, _NA_QROWS, _NA_KROWS, w, w)
    rvalid = np.asarray(rvalids).reshape(ncase, _NA_QROWS, _NA_KROWS)
    valid = rvalid[:, :, :, None, None] & cvalid[None, None, None]
    tab = jnp.where(valid[None], tab * LOG2E, NEG_INF)
    return tab.transpose(0, 1, 3, 5, 2, 4).reshape(nh, ncase, nk, nq)


def _na_kernel(q_ref, k_ref, v_ref, bias_ref, o_ref, s0, s1, p0, p1, l0, l1, *, offsets):
    seq = q_ref.shape[0]
    rows = seq // GRID_W
    nq = _NA_QROWS * GRID_W
    nk = _NA_KROWS * GRID_W
    nblk = rows // _NA_QROWS
    s_buf, p_buf, l_buf = (s0, s1), (p0, p1), (l0, l1)

    def geom(blk):
        blk = jnp.minimum(blk, nblk - 1)
        i0 = blk * _NA_QROWS
        ks = jnp.clip(i0 - NA_ROWS // 2, 0, rows - _NA_KROWS)
        case = sum((i0 - ks >= off).astype(jnp.int32) for off in offsets[1:])
        qrows = pl.ds(pl.multiple_of(i0 * GRID_W, nq), nq)
        krows = pl.ds(pl.multiple_of(ks * GRID_W, GRID_W), nk)
        return qrows, krows, case

    def scores(blk, slot):
        qrows, krows, case = geom(blk)
        st = lax.dot_general(k_ref[krows, :], q_ref[qrows, :], (((1,), (1,)), ((), ())),
                             preferred_element_type=F32)
        s_buf[slot][...] = st + bias_ref[case]

    def softmax(slot):
        s = s_buf[slot]
        nchunk = nk // _NA_CHUNK
        m = s[0:_NA_CHUNK, :]
        for c in range(1, nchunk):
            m = jnp.maximum(m, s[c * _NA_CHUNK:(c + 1) * _NA_CHUNK, :])
        m = jnp.max(m, axis=0, keepdims=True)
        l = jnp.zeros((_NA_CHUNK, nq), F32)
        for c in range(nchunk):
            p = jnp.exp2(s[c * _NA_CHUNK:(c + 1) * _NA_CHUNK, :] - m)
            l = l + p
            p_buf[slot][c * _NA_CHUNK:(c + 1) * _NA_CHUNK, :] = p.astype(BF16)
        l_buf[slot][...] = jnp.broadcast_to(jnp.sum(l, axis=0, keepdims=True), l_buf[slot].shape)

    def values(blk, slot):
        qrows, krows, _ = geom(blk)
        ot = lax.dot_general(v_ref[krows, :], p_buf[slot][...], (((0,), (0,)), ((), ())),
                             preferred_element_type=F32)
        ot = ot * (1.0 / l_buf[slot][0:1, :])
        o_ref[qrows, :] = ot.T.astype(BF16)

    scores(0, 0)
    scores(1, 1)
    softmax(0)

    def body(j, carry):
        w = 2 * j
        scores(w + 2, 0)
        softmax(1)
        values(w, 0)
        scores(w + 3, 1)
        softmax(0)
        values(w + 1, 1)
        return carry

    lax.fori_loop(0, nblk // 2, body, 0)


def _neighborhood_attention(qkv, bias, offsets):
    bsz, _, seq, hd = qkv.shape
    nh = N_HEADS_NA
    ncase, nk, nq = bias.shape[1:]

    def spec(off):
        return pl.BlockSpec((None, None, seq, hd), lambda b, h: (b, off + h, 0, 0))

    return pl.pallas_call(
        functools.partial(_na_kernel, offsets=offsets),
        grid=(bsz, nh),
        in_specs=[spec(0), spec(nh), spec(2 * nh),
                  pl.BlockSpec((None, ncase, nk, nq), lambda b, h: (h, 0, 0, 0))],
        out_specs=pl.BlockSpec((None, None, seq, hd), lambda b, h: (b, h, 0, 0)),
        out_shape=jax.ShapeDtypeStruct((bsz, nh, seq, hd), BF16),
        scratch_shapes=[pltpu.VMEM((nk, nq), F32), pltpu.VMEM((nk, nq), F32),
                        pltpu.VMEM((nk, nq), BF16), pltpu.VMEM((nk, nq), BF16),
                        pltpu.VMEM((8, nq), F32), pltpu.VMEM((8, nq), F32)],
        compiler_params=_cparams(("arbitrary", "arbitrary")),
        name="neighborhood_attention",
    )(qkv, qkv, qkv, bias)


def _out_kernel(a_ref, b_ref, x_ref, mod_ref, w_ref, o_ref, mix_s):
    na = a_ref.shape[0]
    for h in range(na):
        mix_s[:, h * HEAD_DIM:(h + 1) * HEAD_DIM] = a_ref[h]
    for h in range(b_ref.shape[0]):
        mix_s[:, (na + h) * HEAD_DIM:(na + h + 1) * HEAD_DIM] = b_ref[h]
    y = jnp.dot(mix_s[...], w_ref[...], preferred_element_type=F32)
    o_ref[...] = x_ref[...] + mod_ref[2:3, :] * y


def _out_proj(att_a, att_b, x, mod, w, tm=512):
    bsz, seq, d = x.shape
    na, nb = att_a.shape[1], att_b.shape[1]
    return pl.pallas_call(
        _out_kernel,
        grid=(bsz, seq // tm),
        in_specs=[
            pl.BlockSpec((None, na, tm, HEAD_DIM), lambda b, i: (b, 0, i, 0)),
            pl.BlockSpec((None, nb, tm, HEAD_DIM), lambda b, i: (b, 0, i, 0)),
            pl.BlockSpec((None, tm, d), lambda b, i: (b, i, 0)),
            pl.BlockSpec((None, 6, d), lambda b, i: (b, 0, 0)),
            pl.BlockSpec((d, d), lambda b, i: (0, 0), pipeline_mode=pl.Buffered(1)),
        ],
        out_specs=pl.BlockSpec((None, tm, d), lambda b, i: (b, i, 0)),
        out_shape=jax.ShapeDtypeStruct((bsz, seq, d), F32),
        scratch_shapes=[pltpu.VMEM((tm, d), BF16)],
        compiler_params=_cparams(("arbitrary", "arbitrary")),
        name="attn_out_proj",
    )(att_a, att_b, x, mod, w)


def _mlp_in_kernel(x_ref, mod_ref, ln_ref, w_ref, o_ref, h_s):
    @pl.when(pl.program_id(2) == 0)
    def _():
        _norm_modulate(x_ref, ln_ref, mod_ref[4:5, :], mod_ref[3:4, :], h_s)

    y = jnp.maximum(jnp.dot(h_s[...], w_ref[...], preferred_element_type=F32), 0.0)
    o_ref[...] = (y * y).astype(BF16)


def _mlp_in(x, mod, ln, w, tm=1024, tn=1024):
    bsz, seq, d = x.shape
    f = w.shape[1]
    return pl.pallas_call(
        _mlp_in_kernel,
        grid=(bsz, seq // tm, f // tn),
        in_specs=[
            pl.BlockSpec((None, tm, d), lambda b, i, j: (b, i, 0)),
            pl.BlockSpec((None, 6, d), lambda b, i, j: (b, 0, 0)),
            pl.BlockSpec((1, d), lambda b, i, j: (0, 0)),
            pl.BlockSpec((d, tn), lambda b, i, j: (0, j)),
        ],
        out_specs=pl.BlockSpec((None, tm, tn), lambda b, i, j: (b, i, j)),
        out_shape=jax.ShapeDtypeStruct((bsz, seq, f), BF16),
        scratch_shapes=[pltpu.VMEM((tm, d), BF16)],
        compiler_params=_cparams(("arbitrary", "arbitrary", "arbitrary")),
        name="mlp_in",
    )(x, mod, ln, w)


def _mlp_out_kernel(h_ref, x_ref, mod_ref, w_ref, o_ref):
    y = jnp.dot(h_ref[...], w_ref[...], preferred_element_type=F32)
    o_ref[...] = x_ref[...] + mod_ref[5:6, :] * y


def _mlp_out(hid, x, mod, w, tm=512, tn=512):
    bsz, seq, d = x.shape
    f = hid.shape[2]
    return pl.pallas_call(
        _mlp_out_kernel,
        grid=(bsz, seq // tm, d // tn),
        in_specs=[
            pl.BlockSpec((None, tm, f), lambda b, i, j: (b, i, 0)),
            pl.BlockSpec((None, tm, tn), lambda b, i, j: (b, i, j)),
            pl.BlockSpec((None, 6, tn), lambda b, i, j: (b, 0, j)),
            pl.BlockSpec((f, tn), lambda b, i, j: (0, j)),
        ],
        out_specs=pl.BlockSpec((None, tm, tn), lambda b, i, j: (b, i, j)),
        out_shape=jax.ShapeDtypeStruct((bsz, seq, d), F32),
        compiler_params=_cparams(("arbitrary", "arbitrary", "arbitrary")),
        name="mlp_out",
    )(hid, x, mod, w)


def _forward(x, c, ln1, w_ada, b_ada, w_in, q_norm_dil, k_norm_dil, q_norm_na, k_norm_na,
             na_rel_bias, w_out, ln2, w_mlp_in, w_mlp_out, dil_tq):
    bsz, seq, d = x.shape
    depth = w_ada.shape[0]
    rows = seq // GRID_W
    tables = _rope_tables(seq)
    mod_all = _modulation(c, w_ada, b_ada).reshape(depth, bsz, 6, d)
    for layer in range(depth):
        mod = mod_all[layer]
        gains = jnp.stack([q_norm_dil[layer], k_norm_dil[layer],
                           q_norm_na[layer], k_norm_na[layer]]).astype(F32)
        qkv_dil, qkv_na = _qkv_proj(x, mod, ln1[layer][None, :], w_in[layer].astype(BF16),
                                    gains, tables)
        att_dil = _dilated_attention(qkv_dil, tq=dil_tq)
        att_na = _neighborhood_attention(qkv_na, _na_bias_tables(na_rel_bias[layer], rows),
                                         _na_offsets(rows))
        x = _out_proj(att_dil, att_na, x, mod, w_out[layer].astype(BF16))
        hid = _mlp_in(x, mod, ln2[layer][None, :], w_mlp_in[layer].astype(BF16))
        x = _mlp_out(hid, x, mod, w_mlp_out[layer].astype(BF16))
    return x


def kernel(x, c, ln1, w_ada, b_ada, w_in, q_norm_dil, k_norm_dil, q_norm_na, k_norm_na,
           na_rel_bias, w_out, ln2, w_mlp_in, w_mlp_out):
    return _forward(x, c, ln1, w_ada, b_ada, w_in, q_norm_dil, k_norm_dil, q_norm_na,
                    k_norm_na, na_rel_bias, w_out, ln2, w_mlp_in, w_mlp_out, dil_tq=256)
```

```python
import functools

import numpy as np
import jax
import jax.numpy as jnp
from jax import lax
from jax.experimental import pallas as pl
from jax.experimental.pallas import tpu as pltpu

HEAD_DIM = 128
N_HEADS_DIL = 8
N_HEADS_NA = 8
ROT_DIM = HEAD_DIM // 4
ROPE_THETA = 500000.0
DIL_HALF = 64
DILATIONS = (16, 4, 1)
GRID_W = 64
NA_ROWS = 8
NA_COLS = 16
EPS = 1e-6
NEG_INF = -1e30
LOG2E = float(np.log2(np.e))

V7X_VMEM_BYTES = 64 * 1024 * 1024
VMEM_LIMIT = 56 * 1024 * 1024
DIL_VMEM_LIMIT = 60 * 1024 * 1024

F32 = jnp.float32
BF16 = jnp.bfloat16


def _cparams(sem):
    return pltpu.CompilerParams(dimension_semantics=sem, vmem_limit_bytes=VMEM_LIMIT)


def _mod_kernel(ct_ref, w_ref, b_ref, o_ref, act_s, *, nb):
    d = w_ref.shape[0]
    tn = w_ref.shape[1]

    @pl.when((pl.program_id(0) == 0) & (pl.program_id(1) == 0))
    def _():
        ct = ct_ref[...]
        act = ct * (1.0 / (1.0 + jnp.exp(-ct)))
        for b in range(nb):
            act_s[b] = jnp.broadcast_to(act[:, b:b + 1], (d, HEAD_DIM))

    def body(k, accs):
        r0 = pl.multiple_of(k * 8, 8)
        w8 = w_ref[pl.ds(r0, 8), :]
        out = []
        for b, acc in enumerate(accs):
            a = act_s[b, pl.ds(r0, 8), :]
            out.append(acc + w8 * jnp.concatenate([a] * (tn // HEAD_DIM), axis=1))
        return tuple(out)

    accs = lax.fori_loop(0, d // 8, body, tuple(jnp.zeros((8, tn), F32) for _ in range(nb)),
                         unroll=4)
    for b in range(nb):
        o_ref[b:b + 1, :] = jnp.sum(accs[b], axis=0, keepdims=True) + b_ref[...]


def _modulation(c, w_ada, b_ada):
    nl, d, n6 = w_ada.shape
    nb = c.shape[0]
    tn = 1024
    return pl.pallas_call(
        functools.partial(_mod_kernel, nb=nb),
        grid=(nl, n6 // tn),
        in_specs=[
            pl.BlockSpec((d, nb), lambda l, j: (0, 0)),
            pl.BlockSpec((None, d, tn), lambda l, j: (l, 0, j)),
            pl.BlockSpec((None, 1, tn), lambda l, j: (l, 0, j)),
        ],
        out_specs=pl.BlockSpec((None, nb, tn), lambda l, j: (l, 0, j)),
        out_shape=jax.ShapeDtypeStruct((nl, nb, n6), F32),
        scratch_shapes=[pltpu.VMEM((nb, d, HEAD_DIM), F32)],
        compiler_params=_cparams(("arbitrary", "arbitrary")),
        name="adaln_modulation",
    )(c.T, w_ada, b_ada.reshape(nl, 1, n6))


_NORM_CHUNK = 64


def _norm_modulate(x_ref, ln_ref, sc, sh, h_s):
    tm = x_ref.shape[0]
    gain = ln_ref[...] * (1.0 + sc)

    def body(c, carry):
        r0 = pl.multiple_of(c * _NORM_CHUNK, _NORM_CHUNK)
        x = x_ref[pl.ds(r0, _NORM_CHUNK), :]
        ms = jnp.mean(x * x, axis=-1, keepdims=True)
        h_s[pl.ds(r0, _NORM_CHUNK), :] = (x * lax.rsqrt(ms + EPS) * gain + sh).astype(BF16)
        return carry

    lax.fori_loop(0, tm // _NORM_CHUNK, body, 0)


def _qkv_kernel(x_ref, mod_ref, ln_ref, perm_ref, w_ref, gain_ref, cos_ref, sa_ref, sb_ref,
                oc_ref, on_ref, hn_s, hc_s, acc_s):
    scale = HEAD_DIM ** -0.5 * LOG2E
    nh = N_HEADS_DIL
    tn = nh * HEAD_DIM
    tm = x_ref.shape[0]
    ncls = oc_ref.shape[1]
    _norm_modulate(x_ref, ln_ref, mod_ref[1:2, :], mod_ref[0:1, :], hn_s)
    hc_s[...] = jnp.dot(perm_ref[...], hn_s[...], preferred_element_type=F32).astype(BF16)

    for g in range(w_ref.shape[1] // tn):
        rope = g < 3
        h_s = hc_s if rope else hn_s
        acc = acc_s.at[g % 2]
        acc[...] = jnp.dot(h_s[...], w_ref[:, g * tn:(g + 1) * tn], preferred_element_type=F32)
        kind = g % 3
        if kind < 2:
            row = 2 * (g // 3) + kind
            gain = gain_ref[row:row + 1, :] * (scale if kind == 0 else 1.0)
        for h in range(nh):
            y = acc[:, h * HEAD_DIM:(h + 1) * HEAD_DIM]
            if kind < 2:
                ms = jnp.mean(y * y, axis=-1, keepdims=True)
                y = y * lax.rsqrt(ms + EPS) * gain
            if rope and kind < 2:
                y = (y * cos_ref[...]
                     + pltpu.roll(y, ROT_DIM // 2, 1) * sa_ref[...]
                     + pltpu.roll(y, HEAD_DIM - ROT_DIM // 2, 1) * sb_ref[...])
            if rope:
                oc_ref[kind * nh + h] = y.astype(BF16).reshape(ncls, tm // ncls, HEAD_DIM)
            else:
                on_ref[kind * nh + h] = y.astype(BF16)


def _class_major_positions(seq, tm):
    dmax = DILATIONS[0]
    pos = np.arange(seq).reshape(seq // tm, tm // dmax, dmax)
    return pos.transpose(0, 2, 1).reshape(seq)


def _rope_tables(seq, tm):
    half = ROT_DIM // 2
    pos = _class_major_positions(seq, tm).astype(np.float64)[:, None]
    inv = ROPE_THETA ** (-np.arange(0, ROT_DIM, 2, dtype=np.float64) / ROT_DIM)[None, :]
    ang = pos * inv
    cos = np.ones((seq, HEAD_DIM), np.float64)
    sa = np.zeros((seq, HEAD_DIM), np.float64)
    sb = np.zeros((seq, HEAD_DIM), np.float64)
    cos[:, :half] = np.cos(ang)
    cos[:, half:ROT_DIM] = np.cos(ang)
    sa[:, half:ROT_DIM] = np.sin(ang)
    sb[:, :half] = -np.sin(ang)
    return (jnp.asarray(cos, F32), jnp.asarray(sa, F32), jnp.asarray(sb, F32))


_QKV_TM = 256


def _qkv_proj(x, mod, ln, w, gains, tables, tm=_QKV_TM):
    bsz, seq, d = x.shape
    n = w.shape[1]
    tn = N_HEADS_DIL * HEAD_DIM
    nslab = n // HEAD_DIM // 2
    dmax = DILATIONS[0]
    cos, sa, sb = tables
    perm = _class_major_positions(tm, tm)
    perm_mat = jnp.asarray(np.eye(tm)[perm], BF16)
    tab_spec = pl.BlockSpec((tm, HEAD_DIM), lambda b, t: (t, 0))
    qkv_dil, qkv_na = pl.pallas_call(
        _qkv_kernel,
        grid=(bsz, seq // tm),
        in_specs=[
            pl.BlockSpec((None, tm, d), lambda b, t: (b, t, 0)),
            pl.BlockSpec((None, 6, d), lambda b, t: (b, 0, 0)),
            pl.BlockSpec((1, d), lambda b, t: (0, 0)),
            pl.BlockSpec((tm, tm), lambda b, t: (0, 0)),
            pl.BlockSpec((d, n), lambda b, t: (0, 0), pipeline_mode=pl.Buffered(1)),
            pl.BlockSpec((4, HEAD_DIM), lambda b, t: (0, 0)),
            tab_spec, tab_spec, tab_spec,
        ],
        out_specs=[
            pl.BlockSpec((None, nslab, dmax, tm // dmax, HEAD_DIM), lambda b, t: (b, 0, 0, t, 0)),
            pl.BlockSpec((None, nslab, tm, HEAD_DIM), lambda b, t: (b, 0, t, 0)),
        ],
        out_shape=[jax.ShapeDtypeStruct((bsz, nslab, dmax, seq // dmax, HEAD_DIM), BF16),
                   jax.ShapeDtypeStruct((bsz, nslab, seq, HEAD_DIM), BF16)],
        scratch_shapes=[pltpu.VMEM((tm, d), BF16), pltpu.VMEM((tm, d), BF16),
                        pltpu.VMEM((2, tm, tn), F32)],
        compiler_params=_cparams(("arbitrary", "arbitrary")),
        name="qkv_proj",
    )(x, mod, ln, perm_mat, w, gains, cos, sa, sb)
    return qkv_dil, qkv_na


_CAST_CHUNK = 256


def _dil_mask_tables(tq):
    kw = tq + 2 * DIL_HALF
    i = np.arange(tq)[:, None]
    j = np.arange(kw)[None, :]
    tabs = [np.where(np.abs(j - i - DIL_HALF * case) <= DIL_HALF, 0.0, NEG_INF) for case in range(3)]
    return jnp.asarray(np.stack(tabs), F32)


def _dil_kernel(q_ref, k_ref, v_ref, mask_ref, o_ref, a0, a1, a2, b0, b1, b2, c0, c1, c2,
                s0, s1, p0, p1, vb0, vb1, mb0, mb1, lb0, lb1, *, tq):
    seq = o_ref.shape[0]
    kw = tq + 2 * DIL_HALF
    s_buf, p_buf, v_buf, m_buf, l_buf = (s0, s1), (p0, p1), (vb0, vb1), (mb0, mb1), (lb0, lb1)
    set_a, set_b, set_c = (a0, a1, a2), (b0, b1, b2), (c0, c1, c2)
    step = DILATIONS[0] // DILATIONS[1]

    def finer_rows(dil, r, n0, size):
        fine = dil // step
        return pl.ds((r % fine) * (seq // fine) + r // fine + step * n0, size, stride=step)

    def class_rows(ref, cls_len, r, start, size, align):
        if len(ref.shape) == 3:
            return ref[r, pl.ds(pl.multiple_of(start, align), size), :]
        return ref[pl.ds(pl.multiple_of(r * cls_len + start, align), size), :]

    def reorder(dil, src, dst):
        cls_len = seq // dil
        nchunk = cls_len // _CAST_CHUNK

        def body(it, carry):
            r = it // nchunk
            n0 = (it % nchunk) * _CAST_CHUNK
            for s_ref, d_ref in zip(src, dst):
                chunk = class_rows(s_ref, cls_len, r, n0, _CAST_CHUNK, _CAST_CHUNK)
                d_ref[finer_rows(dil, r, n0, _CAST_CHUNK), :] = chunk.astype(F32)
            return carry

        lax.fori_loop(0, dil * nchunk, body, 0)

    def run_pattern(dil, mode, src, acc_in, acc_out):
        cls_len = seq // dil
        nblk = cls_len // tq
        total = dil * nblk
        qs, ks, vs = src

        def geom(it):
            it = jnp.minimum(it, total - 1)
            r = it // nblk
            n0 = (it % nblk) * tq
            nk0 = jnp.clip(n0 - DIL_HALF, 0, cls_len - kw)
            qrows = pl.ds(pl.multiple_of(r * cls_len + n0, tq), tq)
            return qrows, nk0, (n0 - nk0) // DIL_HALF, r, n0

        def scores(it, slot):
            _, nk0, case, r, n0 = geom(it)
            q = class_rows(qs, cls_len, r, n0, tq, tq).astype(BF16)
            k = class_rows(ks, cls_len, r, nk0, kw, DIL_HALF).astype(BF16)
            v_buf[slot][...] = class_rows(vs, cls_len, r, nk0, kw, DIL_HALF).astype(BF16)
            s = lax.dot_general(q, k, (((1,), (1,)), ((), ())), preferred_element_type=F32)
            s_buf[slot][...] = s + mask_ref[case]

        def softmax(slot):
            s = s_buf[slot][...]
            m = jnp.max(s, axis=-1, keepdims=True)
            p = jnp.exp2(s - m)
            l = jnp.sum(p, axis=-1, keepdims=True)
            p_buf[slot][...] = p.astype(BF16)
            m_buf[slot][...] = jnp.broadcast_to(m, (tq, HEAD_DIM))
            l_buf[slot][...] = jnp.broadcast_to(l, (tq, HEAD_DIM))

        def values(it, slot):
            qrows, _, _, r, n0 = geom(it)
            o = jnp.dot(p_buf[slot][...], v_buf[slot][...], preferred_element_type=F32)
            mb = m_buf[slot][...]
            lb = l_buf[slot][...]
            if mode != "init":
                acc_i, m_i, l_i = acc_in
                m_old = m_i[qrows, :]
                m_new = jnp.maximum(m_old, mb)
                a_old = jnp.exp2(m_old - m_new)
                a_new = jnp.exp2(mb - m_new)
                o = acc_i[qrows, :] * a_old + o * a_new
                lb = l_i[qrows, :] * a_old + lb * a_new
                mb = m_new
            if mode == "final":
                o_ref[qrows, :] = (o / lb).astype(BF16)
            else:
                acc_o, m_o, l_o = acc_out
                orows = finer_rows(dil, r, n0, tq)
                acc_o[orows, :] = o
                m_o[orows, :] = mb
                l_o[orows, :] = lb

        scores(0, 0)
        scores(1, 1)
        softmax(0)

        def body(j, carry):
            w = 2 * j
            values(w, 0)
            scores(w + 2, 0)
            softmax(1)
            values(w + 1, 1)
            scores(w + 3, 1)
            softmax(0)
            return carry

        lax.fori_loop(0, total // 2, body, 0)

    d16, d4, d1 = DILATIONS
    run_pattern(d16, "init", (q_ref, k_ref, v_ref), None, set_a)
    reorder(d16, (q_ref, k_ref, v_ref), set_b)
    run_pattern(d4, "merge", set_b, set_a, set_c)
    reorder(d4, set_b, set_a)
    run_pattern(d1, "final", set_a, set_c, None)


def _dilated_attention(qkv, tq=256):
    bsz, _, ncls, cls_len, hd = qkv.shape
    seq = ncls * cls_len
    nh = N_HEADS_DIL
    kw = tq + 2 * DIL_HALF
    masks = _dil_mask_tables(tq)

    def spec(off):
        return pl.BlockSpec((None, None, ncls, cls_len, hd), lambda b, h: (b, off + h, 0, 0, 0))

    def two(shape, dtype):
        return [pltpu.VMEM(shape, dtype), pltpu.VMEM(shape, dtype)]

    return pl.pallas_call(
        functools.partial(_dil_kernel, tq=tq),
        grid=(bsz, nh),
        in_specs=[spec(0), spec(nh), spec(2 * nh),
                  pl.BlockSpec(masks.shape, lambda b, h: (0, 0, 0), pipeline_mode=pl.Buffered(1))],
        out_specs=pl.BlockSpec((None, None, seq, hd), lambda b, h: (b, h, 0, 0)),
        out_shape=jax.ShapeDtypeStruct((bsz, nh, seq, hd), BF16),
        scratch_shapes=([pltpu.VMEM((seq, hd), F32) for _ in range(9)]
                        + two((tq, kw), F32) + two((tq, kw), BF16) + two((kw, hd), BF16)
                        + two((tq, hd), F32) + two((tq, hd), F32)),
        compiler_params=pltpu.CompilerParams(dimension_semantics=("arbitrary", "arbitrary"),
                                             vmem_limit_bytes=DIL_VMEM_LIMIT),
        name="dilated_attention",
    )(qkv, qkv, qkv, masks)


_NA_QROWS = 4
_NA_KROWS = NA_ROWS + _NA_QROWS - 1
_NA_CHUNK = 64


def _na_key_start(i0, rows):
    return np.clip(i0 - NA_ROWS // 2, 0, rows - _NA_KROWS)


def _na_offsets(rows):
    return tuple(sorted({int(i0 - _na_key_start(i0, rows)) for i0 in range(0, rows, _NA_QROWS)}))


def _na_bias_tables(rpb, rows):
    nq = _NA_QROWS * GRID_W
    nk = _NA_KROWS * GRID_W
    nh, nrow, ncol = rpb.shape
    w = GRID_W
    u = jnp.pad(rpb.astype(F32), ((0, 0), (0, 0), (w - NA_COLS, w + NA_COLS - ncol)))
    toep = jnp.broadcast_to(u[:, :, None, :], (nh, nrow, w, 2 * w)).reshape(nh, nrow, 2 * w * w)
    toep = toep[:, :, :w * (2 * w - 1)].reshape(nh, nrow, w, 2 * w - 1)[:, :, :, w - 1:]
    cs = np.clip(np.arange(w) - NA_COLS // 2, 0, w - NA_COLS)
    cvalid = (np.arange(w)[None, :] >= cs[:, None]) & (np.arange(w)[None, :] < cs[:, None] + NA_COLS)
    def row_pattern(i0):
        ks = int(_na_key_start(i0, rows))
        roff, rvalid = [], []
        for qi in range(i0, i0 + _NA_QROWS):
            rs = int(np.clip(qi - NA_ROWS // 2, 0, rows - NA_ROWS))
            for kr in range(ks, ks + _NA_KROWS):
                roff.append(int(np.clip(kr - qi + NA_ROWS - 1, 0, nrow - 1)))
                rvalid.append(rs <= kr < rs + NA_ROWS)
        return roff, rvalid

    slabs, rvalids = [], []
    for off in _na_offsets(rows):
        blocks = [i0 for i0 in range(0, rows, _NA_QROWS) if i0 - _na_key_start(i0, rows) == off]
        roff, rvalid = row_pattern(blocks[0])
        assert all(row_pattern(i0) == (roff, rvalid) for i0 in blocks)
        slabs += [toep[:, r] for r in roff]
        rvalids += rvalid
    ncase = len(_na_offsets(rows))
    tab = jnp.stack(slabs, axis=1).reshape(nh, ncase, _NA_QROWS, _NA_KROWS, w, w)
    rvalid = np.asarray(rvalids).reshape(ncase, _NA_QROWS, _NA_KROWS)
    valid = rvalid[:, :, :, None, None] & cvalid[None, None, None]
    tab = jnp.where(valid[None], tab * LOG2E, NEG_INF)
    return tab.transpose(0, 1, 3, 5, 2, 4).reshape(nh, ncase, nk, nq)


def _na_kernel(q_ref, k_ref, v_ref, bias_ref, o_ref, s0, s1, p0, p1, l0, l1, *, offsets):
    seq = q_ref.shape[0]
    rows = seq // GRID_W
    nq = _NA_QROWS * GRID_W
    nk = _NA_KROWS * GRID_W
    nblk = rows // _NA_QROWS
    s_buf, p_buf, l_buf = (s0, s1), (p0, p1), (l0, l1)

    def geom(blk):
        blk = jnp.minimum(blk, nblk - 1)
        i0 = blk * _NA_QROWS
        ks = jnp.clip(i0 - NA_ROWS // 2, 0, rows - _NA_KROWS)
        case = sum((i0 - ks >= off).astype(jnp.int32) for off in offsets[1:])
        qrows = pl.ds(pl.multiple_of(i0 * GRID_W, nq), nq)
        krows = pl.ds(pl.multiple_of(ks * GRID_W, GRID_W), nk)
        return qrows, krows, case

    def scores(blk, slot):
        qrows, krows, case = geom(blk)
        st = lax.dot_general(k_ref[krows, :], q_ref[qrows, :], (((1,), (1,)), ((), ())),
                             preferred_element_type=F32)
        s_buf[slot][...] = st + bias_ref[case]

    def softmax(slot):
        s = s_buf[slot]
        nchunk = nk // _NA_CHUNK
        m = s[0:_NA_CHUNK, :]
        for c in range(1, nchunk):
            m = jnp.maximum(m, s[c * _NA_CHUNK:(c + 1) * _NA_CHUNK, :])
        m = jnp.max(m, axis=0, keepdims=True)
        l = jnp.zeros((_NA_CHUNK, nq), F32)
        for c in range(nchunk):
            p = jnp.exp2(s[c * _NA_CHUNK:(c + 1) * _NA_CHUNK, :] - m)
            l = l + p
            p_buf[slot][c * _NA_CHUNK:(c + 1) * _NA_CHUNK, :] = p.astype(BF16)
        l_buf[slot][...] = jnp.broadcast_to(jnp.sum(l, axis=0, keepdims=True), l_buf[slot].shape)

    def values(blk, slot):
        qrows, krows, _ = geom(blk)
        ot = lax.dot_general(v_ref[krows, :], p_buf[slot][...], (((0,), (0,)), ((), ())),
                             preferred_element_type=F32)
        ot = ot * (1.0 / l_buf[slot][0:1, :])
        o_ref[qrows, :] = ot.T.astype(BF16)

    scores(0, 0)
    scores(1, 1)
    softmax(0)

    def body(j, carry):
        w = 2 * j
        scores(w + 2, 0)
        softmax(1)
        values(w, 0)
        scores(w + 3, 1)
        softmax(0)
        values(w + 1, 1)
        return carry

    lax.fori_loop(0, nblk // 2, body, 0)


def _neighborhood_attention(qkv, bias, offsets):
    bsz, _, seq, hd = qkv.shape
    nh = N_HEADS_NA
    ncase, nk, nq = bias.shape[1:]

    def spec(off):
        return pl.BlockSpec((None, None, seq, hd), lambda b, h: (b, off + h, 0, 0))

    return pl.pallas_call(
        functools.partial(_na_kernel, offsets=offsets),
        grid=(bsz, nh),
        in_specs=[spec(0), spec(nh), spec(2 * nh),
                  pl.BlockSpec((None, ncase, nk, nq), lambda b, h: (h, 0, 0, 0))],
        out_specs=pl.BlockSpec((None, None, seq, hd), lambda b, h: (b, h, 0, 0)),
        out_shape=jax.ShapeDtypeStruct((bsz, nh, seq, hd), BF16),
        scratch_shapes=[pltpu.VMEM((nk, nq), F32), pltpu.VMEM((nk, nq), F32),
                        pltpu.VMEM((nk, nq), BF16), pltpu.VMEM((nk, nq), BF16),
                        pltpu.VMEM((8, nq), F32), pltpu.VMEM((8, nq), F32)],
        compiler_params=_cparams(("arbitrary", "arbitrary")),
        name="neighborhood_attention",
    )(qkv, qkv, qkv, bias)


def _out_kernel(a_ref, b_ref, x_ref, mod_ref, w_ref, o_ref, mix_s):
    na = a_ref.shape[0]
    for h in range(na):
        mix_s[:, h * HEAD_DIM:(h + 1) * HEAD_DIM] = a_ref[h]
    for h in range(b_ref.shape[0]):
        mix_s[:, (na + h) * HEAD_DIM:(na + h + 1) * HEAD_DIM] = b_ref[h]
    y = jnp.dot(mix_s[...], w_ref[...], preferred_element_type=F32)
    o_ref[...] = x_ref[...] + mod_ref[2:3, :] * y


def _out_proj(att_a, att_b, x, mod, w, tm=512):
    bsz, seq, d = x.shape
    na, nb = att_a.shape[1], att_b.shape[1]
    return pl.pallas_call(
        _out_kernel,
        grid=(bsz, seq // tm),
        in_specs=[
            pl.BlockSpec((None, na, tm, HEAD_DIM), lambda b, i: (b, 0, i, 0)),
            pl.BlockSpec((None, nb, tm, HEAD_DIM), lambda b, i: (b, 0, i, 0)),
            pl.BlockSpec((None, tm, d), lambda b, i: (b, i, 0)),
            pl.BlockSpec((None, 6, d), lambda b, i: (b, 0, 0)),
            pl.BlockSpec((d, d), lambda b, i: (0, 0), pipeline_mode=pl.Buffered(1)),
        ],
        out_specs=pl.BlockSpec((None, tm, d), lambda b, i: (b, i, 0)),
        out_shape=jax.ShapeDtypeStruct((bsz, seq, d), F32),
        scratch_shapes=[pltpu.VMEM((tm, d), BF16)],
        compiler_params=_cparams(("arbitrary", "arbitrary")),
        name="attn_out_proj",
    )(att_a, att_b, x, mod, w)


def _mlp_in_kernel(x_ref, mod_ref, ln_ref, w_ref, o_ref, h_s):
    @pl.when(pl.program_id(2) == 0)
    def _():
        _norm_modulate(x_ref, ln_ref, mod_ref[4:5, :], mod_ref[3:4, :], h_s)

    y = jnp.maximum(jnp.dot(h_s[...], w_ref[...], preferred_element_type=F32), 0.0)
    o_ref[...] = (y * y).astype(BF16)


def _mlp_in(x, mod, ln, w, tm=1024, tn=1024):
    bsz, seq, d = x.shape
    f = w.shape[1]
    return pl.pallas_call(
        _mlp_in_kernel,
        grid=(bsz, seq // tm, f // tn),
        in_specs=[
            pl.BlockSpec((None, tm, d), lambda b, i, j: (b, i, 0)),
            pl.BlockSpec((None, 6, d), lambda b, i, j: (b, 0, 0)),
            pl.BlockSpec((1, d), lambda b, i, j: (0, 0)),
            pl.BlockSpec((d, tn), lambda b, i, j: (0, j)),
        ],
        out_specs=pl.BlockSpec((None, tm, tn), lambda b, i, j: (b, i, j)),
        out_shape=jax.ShapeDtypeStruct((bsz, seq, f), BF16),
        scratch_shapes=[pltpu.VMEM((tm, d), BF16)],
        compiler_params=_cparams(("arbitrary", "arbitrary", "arbitrary")),
        name="mlp_in",
    )(x, mod, ln, w)


def _mlp_out_kernel(h_ref, x_ref, mod_ref, w_ref, o_ref):
    y = jnp.dot(h_ref[...], w_ref[...], preferred_element_type=F32)
    o_ref[...] = x_ref[...] + mod_ref[5:6, :] * y


def _mlp_out(hid, x, mod, w, tm=512, tn=512):
    bsz, seq, d = x.shape
    f = hid.shape[2]
    return pl.pallas_call(
        _mlp_out_kernel,
        grid=(bsz, seq // tm, d // tn),
        in_specs=[
            pl.BlockSpec((None, tm, f), lambda b, i, j: (b, i, 0)),
            pl.BlockSpec((None, tm, tn), lambda b, i, j: (b, i, j)),
            pl.BlockSpec((None, 6, tn), lambda b, i, j: (b, 0, j)),
            pl.BlockSpec((f, tn), lambda b, i, j: (0, j)),
        ],
        out_specs=pl.BlockSpec((None, tm, tn), lambda b, i, j: (b, i, j)),
        out_shape=jax.ShapeDtypeStruct((bsz, seq, d), F32),
        compiler_params=_cparams(("arbitrary", "arbitrary", "arbitrary")),
        name="mlp_out",
    )(hid, x, mod, w)


def _forward(x, c, ln1, w_ada, b_ada, w_in, q_norm_dil, k_norm_dil, q_norm_na, k_norm_na,
             na_rel_bias, w_out, ln2, w_mlp_in, w_mlp_out, dil_tq):
    bsz, seq, d = x.shape
    depth = w_ada.shape[0]
    rows = seq // GRID_W
    tables = _rope_tables(seq, _QKV_TM)
    mod_all = _modulation(c, w_ada, b_ada).reshape(depth, bsz, 6, d)
    for layer in range(depth):
        mod = mod_all[layer]
        gains = jnp.stack([q_norm_dil[layer], k_norm_dil[layer],
                           q_norm_na[layer], k_norm_na[layer]]).astype(F32)
        qkv_dil, qkv_na = _qkv_proj(x, mod, ln1[layer][None, :], w_in[layer].astype(BF16),
                                    gains, tables)
        att_dil = _dilated_attention(qkv_dil, tq=dil_tq)
        att_na = _neighborhood_attention(qkv_na, _na_bias_tables(na_rel_bias[layer], rows),
                                         _na_offsets(rows))
        x = _out_proj(att_dil, att_na, x, mod, w_out[layer].astype(BF16))
        hid = _mlp_in(x, mod, ln2[layer][None, :], w_mlp_in[layer].astype(BF16))
        x = _mlp_out(hid, x, mod, w_mlp_out[layer].astype(BF16))
    return x


def kernel(x, c, ln1, w_ada, b_ada, w_in, q_norm_dil, k_norm_dil, q_norm_na, k_norm_na,
           na_rel_bias, w_out, ln2, w_mlp_in, w_mlp_out):
    return _forward(x, c, ln1, w_ada, b_ada, w_in, q_norm_dil, k_norm_dil, q_norm_na,
                    k_norm_na, na_rel_bias, w_out, ln2, w_mlp_in, w_mlp_out, dil_tq=256)
```

```python
import functools

import numpy as np
import jax
import jax.numpy as jnp
from jax import lax
from jax.experimental import pallas as pl
from jax.experimental.pallas import tpu as pltpu

HEAD_DIM = 128
N_HEADS_DIL = 8
N_HEADS_NA = 8
ROT_DIM = HEAD_DIM // 4
ROPE_THETA = 500000.0
DIL_HALF = 64
DILATIONS = (16, 4, 1)
GRID_W = 64
NA_ROWS = 8
NA_COLS = 16
EPS = 1e-6
NEG_INF = -1e30
LOG2E = float(np.log2(np.e))

V7X_VMEM_BYTES = 64 * 1024 * 1024
VMEM_LIMIT = 56 * 1024 * 1024
DIL_VMEM_LIMIT = 60 * 1024 * 1024

F32 = jnp.float32
BF16 = jnp.bfloat16


def _cparams(sem):
    return pltpu.CompilerParams(dimension_semantics=sem, vmem_limit_bytes=VMEM_LIMIT)


def _mod_kernel(ct_ref, w_ref, b_ref, o_ref, act_s, *, nb):
    d = w_ref.shape[0]
    tn = w_ref.shape[1]

    @pl.when((pl.program_id(0) == 0) & (pl.program_id(1) == 0))
    def _():
        ct = ct_ref[...]
        act = ct * (1.0 / (1.0 + jnp.exp(-ct)))
        for b in range(nb):
            act_s[b] = jnp.broadcast_to(act[:, b:b + 1], (d, HEAD_DIM))

    def body(k, accs):
        r0 = pl.multiple_of(k * 8, 8)
        w8 = w_ref[pl.ds(r0, 8), :]
        out = []
        for b, acc in enumerate(accs):
            a = act_s[b, pl.ds(r0, 8), :]
            out.append(acc + w8 * jnp.concatenate([a] * (tn // HEAD_DIM), axis=1))
        return tuple(out)

    accs = lax.fori_loop(0, d // 8, body, tuple(jnp.zeros((8, tn), F32) for _ in range(nb)),
                         unroll=4)
    for b in range(nb):
        o_ref[b:b + 1, :] = jnp.sum(accs[b], axis=0, keepdims=True) + b_ref[...]


def _modulation(c, w_ada, b_ada):
    nl, d, n6 = w_ada.shape
    nb = c.shape[0]
    tn = 1024
    return pl.pallas_call(
        functools.partial(_mod_kernel, nb=nb),
        grid=(nl, n6 // tn),
        in_specs=[
            pl.BlockSpec((d, nb), lambda l, j: (0, 0)),
            pl.BlockSpec((None, d, tn), lambda l, j: (l, 0, j)),
            pl.BlockSpec((None, 1, tn), lambda l, j: (l, 0, j)),
        ],
        out_specs=pl.BlockSpec((None, nb, tn), lambda l, j: (l, 0, j)),
        out_shape=jax.ShapeDtypeStruct((nl, nb, n6), F32),
        scratch_shapes=[pltpu.VMEM((nb, d, HEAD_DIM), F32)],
        compiler_params=_cparams(("arbitrary", "arbitrary")),
        name="adaln_modulation",
    )(c.T, w_ada, b_ada.reshape(nl, 1, n6))


_NORM_CHUNK = 64


def _norm_modulate(x_ref, ln_ref, sc, sh, h_s):
    tm = x_ref.shape[0]
    gain = ln_ref[...] * (1.0 + sc)

    def body(c, carry):
        r0 = pl.multiple_of(c * _NORM_CHUNK, _NORM_CHUNK)
        x = x_ref[pl.ds(r0, _NORM_CHUNK), :]
        ms = jnp.mean(x * x, axis=-1, keepdims=True)
        h_s[pl.ds(r0, _NORM_CHUNK), :] = (x * lax.rsqrt(ms + EPS) * gain + sh).astype(BF16)
        return carry

    lax.fori_loop(0, tm // _NORM_CHUNK, body, 0)


def _qkv_kernel(x_ref, mod_ref, ln_ref, perm_ref, w_ref, gain_ref, cos_ref, sa_ref, sb_ref,
                oc_ref, on_ref, hn_s, hc_s, acc_s):
    scale = HEAD_DIM ** -0.5 * LOG2E
    nh = N_HEADS_DIL
    tn = nh * HEAD_DIM
    tm = x_ref.shape[0]
    ncls = oc_ref.shape[1]
    _norm_modulate(x_ref, ln_ref, mod_ref[1:2, :], mod_ref[0:1, :], hn_s)
    hc_s[...] = jnp.dot(perm_ref[...], hn_s[...], preferred_element_type=F32).astype(BF16)

    for g in range(w_ref.shape[1] // tn):
        rope = g < 3
        h_s = hc_s if rope else hn_s
        acc = acc_s.at[g % 2]
        acc[...] = jnp.dot(h_s[...], w_ref[:, g * tn:(g + 1) * tn], preferred_element_type=F32)
        kind = g % 3
        if kind < 2:
            row = 2 * (g // 3) + kind
            gain = gain_ref[row:row + 1, :] * (scale if kind == 0 else 1.0)
        for h in range(nh):
            y = acc[:, h * HEAD_DIM:(h + 1) * HEAD_DIM]
            if kind < 2:
                ms = jnp.mean(y * y, axis=-1, keepdims=True)
                y = y * lax.rsqrt(ms + EPS) * gain
            if rope and kind < 2:
                y = (y * cos_ref[...]
                     + pltpu.roll(y, ROT_DIM // 2, 1) * sa_ref[...]
                     + pltpu.roll(y, HEAD_DIM - ROT_DIM // 2, 1) * sb_ref[...])
            if rope:
                oc_ref[kind * nh + h] = y.astype(BF16).reshape(ncls, tm // ncls, HEAD_DIM)
            else:
                on_ref[kind * nh + h] = y.astype(BF16)


def _class_major_positions(seq, tm):
    dmax = DILATIONS[0]
    pos = np.arange(seq).reshape(seq // tm, tm // dmax, dmax)
    return pos.transpose(0, 2, 1).reshape(seq)


def _rope_tables(seq, tm):
    half = ROT_DIM // 2
    pos = _class_major_positions(seq, tm).astype(np.float64)[:, None]
    inv = ROPE_THETA ** (-np.arange(0, ROT_DIM, 2, dtype=np.float64) / ROT_DIM)[None, :]
    ang = pos * inv
    cos = np.ones((seq, HEAD_DIM), np.float64)
    sa = np.zeros((seq, HEAD_DIM), np.float64)
    sb = np.zeros((seq, HEAD_DIM), np.float64)
    cos[:, :half] = np.cos(ang)
    cos[:, half:ROT_DIM] = np.cos(ang)
    sa[:, half:ROT_DIM] = np.sin(ang)
    sb[:, :half] = -np.sin(ang)
    return (jnp.asarray(cos, F32), jnp.asarray(sa, F32), jnp.asarray(sb, F32))


_QKV_TM = 256


def _qkv_proj(x, mod, ln, w, gains, tables, tm=_QKV_TM):
    bsz, seq, d = x.shape
    n = w.shape[1]
    tn = N_HEADS_DIL * HEAD_DIM
    nslab = n // HEAD_DIM // 2
    dmax = DILATIONS[0]
    cos, sa, sb = tables
    perm = _class_major_positions(tm, tm)
    perm_mat = jnp.asarray(np.eye(tm)[perm], BF16)
    tab_spec = pl.BlockSpec((tm, HEAD_DIM), lambda b, t: (t, 0))
    qkv_dil, qkv_na = pl.pallas_call(
        _qkv_kernel,
        grid=(bsz, seq // tm),
        in_specs=[
            pl.BlockSpec((None, tm, d), lambda b, t: (b, t, 0)),
            pl.BlockSpec((None, 6, d), lambda b, t: (b, 0, 0)),
            pl.BlockSpec((1, d), lambda b, t: (0, 0)),
            pl.BlockSpec((tm, tm), lambda b, t: (0, 0)),
            pl.BlockSpec((d, n), lambda b, t: (0, 0), pipeline_mode=pl.Buffered(1)),
            pl.BlockSpec((4, HEAD_DIM), lambda b, t: (0, 0)),
            tab_spec, tab_spec, tab_spec,
        ],
        out_specs=[
            pl.BlockSpec((None, nslab, dmax, tm // dmax, HEAD_DIM), lambda b, t: (b, 0, 0, t, 0)),
            pl.BlockSpec((None, nslab, tm, HEAD_DIM), lambda b, t: (b, 0, t, 0)),
        ],
        out_shape=[jax.ShapeDtypeStruct((bsz, nslab, dmax, seq // dmax, HEAD_DIM), BF16),
                   jax.ShapeDtypeStruct((bsz, nslab, seq, HEAD_DIM), BF16)],
        scratch_shapes=[pltpu.VMEM((tm, d), BF16), pltpu.VMEM((tm, d), BF16),
                        pltpu.VMEM((2, tm, tn), F32)],
        compiler_params=_cparams(("arbitrary", "arbitrary")),
        name="qkv_proj",
    )(x, mod, ln, perm_mat, w, gains, cos, sa, sb)
    return qkv_dil, qkv_na


_CAST_CHUNK = 256


def _dil_mask_tables(tq):
    kw = tq + 2 * DIL_HALF
    i = np.arange(tq)[:, None]
    j = np.arange(kw)[None, :]
    tabs = [np.where(np.abs(j - i - DIL_HALF * case) <= DIL_HALF, 0.0, NEG_INF) for case in range(3)]
    return jnp.asarray(np.stack(tabs), F32)


def _dil_kernel(q_ref, k_ref, v_ref, mask_ref, o_ref, a0, a1, a2, b0, b1, b2, c0, c1, c2,
                s0, s1, p0, p1, vb0, vb1, mb0, mb1, lb0, lb1, *, tq):
    seq = o_ref.shape[0]
    kw = tq + 2 * DIL_HALF
    s_buf, p_buf, v_buf, m_buf, l_buf = (s0, s1), (p0, p1), (vb0, vb1), (mb0, mb1), (lb0, lb1)
    set_a, set_b, set_c = (a0, a1, a2), (b0, b1, b2), (c0, c1, c2)
    step = DILATIONS[0] // DILATIONS[1]

    def finer_rows(dil, r, n0, size):
        fine = dil // step
        return pl.ds((r % fine) * (seq // fine) + r // fine + step * n0, size, stride=step)

    def class_rows(ref, cls_len, r, start, size, align):
        if len(ref.shape) == 3:
            return ref[r, pl.ds(pl.multiple_of(start, align), size), :]
        return ref[pl.ds(pl.multiple_of(r * cls_len + start, align), size), :]

    def reorder(dil, src, dst):
        cls_len = seq // dil
        nchunk = cls_len // _CAST_CHUNK

        def body(it, carry):
            r = it // nchunk
            n0 = (it % nchunk) * _CAST_CHUNK
            for s_ref, d_ref in zip(src, dst):
                chunk = class_rows(s_ref, cls_len, r, n0, _CAST_CHUNK, _CAST_CHUNK)
                d_ref[finer_rows(dil, r, n0, _CAST_CHUNK), :] = chunk.astype(F32)
            return carry

        lax.fori_loop(0, dil * nchunk, body, 0)

    def run_pattern(dil, mode, src, acc_in, acc_out):
        cls_len = seq // dil
        nblk = cls_len // tq
        total = dil * nblk
        qs, ks, vs = src

        def geom(it):
            it = jnp.minimum(it, total - 1)
            r = it // nblk
            n0 = (it % nblk) * tq
            nk0 = jnp.clip(n0 - DIL_HALF, 0, cls_len - kw)
            qrows = pl.ds(pl.multiple_of(r * cls_len + n0, tq), tq)
            return qrows, nk0, (n0 - nk0) // DIL_HALF, r, n0

        def scores(it, slot):
            _, nk0, case, r, n0 = geom(it)
            q = class_rows(qs, cls_len, r, n0, tq, tq).astype(BF16)
            k = class_rows(ks, cls_len, r, nk0, kw, DIL_HALF).astype(BF16)
            v_buf[slot][...] = class_rows(vs, cls_len, r, nk0, kw, DIL_HALF).astype(BF16)
            s = lax.dot_general(q, k, (((1,), (1,)), ((), ())), preferred_element_type=F32)
            s_buf[slot][...] = s + mask_ref[case]

        def softmax(slot):
            s = s_buf[slot][...]
            m = jnp.max(s, axis=-1, keepdims=True)
            p = jnp.exp2(s - m)
            l = jnp.sum(p, axis=-1, keepdims=True)
            p_buf[slot][...] = p.astype(BF16)
            m_buf[slot][...] = jnp.broadcast_to(m, (tq, HEAD_DIM))
            l_buf[slot][...] = jnp.broadcast_to(l, (tq, HEAD_DIM))

        def values(it, slot):
            qrows, _, _, r, n0 = geom(it)
            o = jnp.dot(p_buf[slot][...], v_buf[slot][...], preferred_element_type=F32)
            mb = m_buf[slot][...]
            lb = l_buf[slot][...]
            if mode != "init":
                acc_i, m_i, l_i = acc_in
                m_old = m_i[qrows, :]
                m_new = jnp.maximum(m_old, mb)
                a_old = jnp.exp2(m_old - m_new)
                a_new = jnp.exp2(mb - m_new)
                o = acc_i[qrows, :] * a_old + o * a_new
                lb = l_i[qrows, :] * a_old + lb * a_new
                mb = m_new
            if mode == "final":
                o_ref[qrows, :] = (o / lb).astype(BF16)
            else:
                acc_o, m_o, l_o = acc_out
                orows = finer_rows(dil, r, n0, tq)
                acc_o[orows, :] = o
                m_o[orows, :] = mb
                l_o[orows, :] = lb

        scores(0, 0)
        scores(1, 1)
        softmax(0)

        def body(j, carry):
            w = 2 * j
            values(w, 0)
            scores(w + 2, 0)
            softmax(1)
            values(w + 1, 1)
            scores(w + 3, 1)
            softmax(0)
            return carry

        lax.fori_loop(0, total // 2, body, 0)

    d16, d4, d1 = DILATIONS
    run_pattern(d16, "init", (q_ref, k_ref, v_ref), None, set_a)
    reorder(d16, (q_ref, k_ref, v_ref), set_b)
    run_pattern(d4, "merge", set_b, set_a, set_c)
    reorder(d4, set_b, set_a)
    run_pattern(d1, "final", set_a, set_c, None)


def _dilated_attention(qkv, tq=256):
    bsz, _, ncls, cls_len, hd = qkv.shape
    seq = ncls * cls_len
    nh = N_HEADS_DIL
    kw = tq + 2 * DIL_HALF
    masks = _dil_mask_tables(tq)

    def spec(off):
        return pl.BlockSpec((None, None, ncls, cls_len, hd), lambda b, h: (b, off + h, 0, 0, 0))

    def two(shape, dtype):
        return [pltpu.VMEM(shape, dtype), pltpu.VMEM(shape, dtype)]

    return pl.pallas_call(
        functools.partial(_dil_kernel, tq=tq),
        grid=(bsz, nh),
        in_specs=[spec(0), spec(nh), spec(2 * nh),
                  pl.BlockSpec(masks.shape, lambda b, h: (0, 0, 0), pipeline_mode=pl.Buffered(1))],
        out_specs=pl.BlockSpec((None, None, seq, hd), lambda b, h: (b, h, 0, 0)),
        out_shape=jax.ShapeDtypeStruct((bsz, nh, seq, hd), BF16),
        scratch_shapes=([pltpu.VMEM((seq, hd), F32) for _ in range(9)]
                        + two((tq, kw), F32) + two((tq, kw), BF16) + two((kw, hd), BF16)
                        + two((tq, hd), F32) + two((tq, hd), F32)),
        compiler_params=pltpu.CompilerParams(dimension_semantics=("arbitrary", "arbitrary"),
                                             vmem_limit_bytes=DIL_VMEM_LIMIT),
        name="dilated_attention",
    )(qkv, qkv, qkv, masks)


_NA_QROWS = 4
_NA_KROWS = NA_ROWS + _NA_QROWS - 1
_NA_CHUNK = 64


def _na_key_start(i0, rows):
    return np.clip(i0 - NA_ROWS // 2, 0, rows - _NA_KROWS)


def _na_offsets(rows):
    return tuple(sorted({int(i0 - _na_key_start(i0, rows)) for i0 in range(0, rows, _NA_QROWS)}))


def _na_bias_tables(rpb, rows):
    nq = _NA_QROWS * GRID_W
    nk = _NA_KROWS * GRID_W
    nh, nrow, ncol = rpb.shape
    w = GRID_W
    u = jnp.pad(rpb.astype(F32) * LOG2E, ((0, 0), (0, 0), (w - NA_COLS, w + NA_COLS - ncol)))
    toep = jnp.broadcast_to(u[:, :, None, :], (nh, nrow, w, 2 * w)).reshape(nh, nrow, 2 * w * w)
    toep = toep[:, :, :w * (2 * w - 1)].reshape(nh, nrow, w, 2 * w - 1)[:, :, :, w - 1:]
    cs = np.clip(np.arange(w) - NA_COLS // 2, 0, w - NA_COLS)
    cvalid = (np.arange(w)[None, :] >= cs[:, None]) & (np.arange(w)[None, :] < cs[:, None] + NA_COLS)
    def row_pattern(i0):
        ks = int(_na_key_start(i0, rows))
        roff, rvalid = [], []
        for qi in range(i0, i0 + _NA_QROWS):
            rs = int(np.clip(qi - NA_ROWS // 2, 0, rows - NA_ROWS))
            for kr in range(ks, ks + _NA_KROWS):
                roff.append(int(np.clip(kr - qi + NA_ROWS - 1, 0, nrow - 1)))
                rvalid.append(rs <= kr < rs + NA_ROWS)
        return roff, rvalid

    slabs, rvalids = [], []
    for off in _na_offsets(rows):
        blocks = [i0 for i0 in range(0, rows, _NA_QROWS) if i0 - _na_key_start(i0, rows) == off]
        roff, rvalid = row_pattern(blocks[0])
        assert all(row_pattern(i0) == (roff, rvalid) for i0 in blocks)
        slabs += [toep[:, r] for r in roff]
        rvalids += rvalid
    ncase = len(_na_offsets(rows))
    tab = jnp.stack(slabs, axis=1).reshape(nh, ncase, _NA_QROWS, _NA_KROWS, w, w)
    rvalid = np.asarray(rvalids).reshape(ncase, _NA_QROWS, _NA_KROWS)
    valid = rvalid[:, :, :, None, None] & cvalid[None, None, None]
    tab = jnp.where(valid[None], tab, NEG_INF)
    return tab.transpose(0, 1, 3, 5, 2, 4).reshape(nh, ncase, nk, nq)


def _na_kernel(q_ref, k_ref, v_ref, bias_ref, o_ref, s0, s1, p0, p1, l0, l1, *, offsets):
    seq = q_ref.shape[0]
    rows = seq // GRID_W
    nq = _NA_QROWS * GRID_W
    nk = _NA_KROWS * GRID_W
    nblk = rows // _NA_QROWS
    s_buf, p_buf, l_buf = (s0, s1), (p0, p1), (l0, l1)

    def geom(blk):
        blk = jnp.minimum(blk, nblk - 1)
        i0 = blk * _NA_QROWS
        ks = jnp.clip(i0 - NA_ROWS // 2, 0, rows - _NA_KROWS)
        case = sum((i0 - ks >= off).astype(jnp.int32) for off in offsets[1:])
        qrows = pl.ds(pl.multiple_of(i0 * GRID_W, nq), nq)
        krows = pl.ds(pl.multiple_of(ks * GRID_W, GRID_W), nk)
        return qrows, krows, case

    def scores(blk, slot):
        qrows, krows, case = geom(blk)
        st = lax.dot_general(k_ref[krows, :], q_ref[qrows, :], (((1,), (1,)), ((), ())),
                             preferred_element_type=F32)
        s_buf[slot][...] = st + bias_ref[case]

    def softmax(slot):
        s = s_buf[slot]
        nchunk = nk // _NA_CHUNK
        m = s[0:_NA_CHUNK, :]
        for c in range(1, nchunk):
            m = jnp.maximum(m, s[c * _NA_CHUNK:(c + 1) * _NA_CHUNK, :])
        m = jnp.max(m, axis=0, keepdims=True)
        l = jnp.zeros((_NA_CHUNK, nq), F32)
        for c in range(nchunk):
            p = jnp.exp2(s[c * _NA_CHUNK:(c + 1) * _NA_CHUNK, :] - m)
            l = l + p
            p_buf[slot][c * _NA_CHUNK:(c + 1) * _NA_CHUNK, :] = p.astype(BF16)
        l_buf[slot][...] = jnp.broadcast_to(jnp.sum(l, axis=0, keepdims=True), l_buf[slot].shape)

    def values(blk, slot):
        qrows, krows, _ = geom(blk)
        ot = lax.dot_general(v_ref[krows, :], p_buf[slot][...], (((0,), (0,)), ((), ())),
                             preferred_element_type=F32)
        ot = ot * (1.0 / l_buf[slot][0:1, :])
        o_ref[qrows, :] = ot.T.astype(BF16)

    scores(0, 0)
    scores(1, 1)
    softmax(0)

    def body(j, carry):
        w = 2 * j
        scores(w + 2, 0)
        softmax(1)
        values(w, 0)
        scores(w + 3, 1)
        softmax(0)
        values(w + 1, 1)
        return carry

    lax.fori_loop(0, nblk // 2, body, 0)


def _neighborhood_attention(qkv, bias, offsets):
    bsz, _, seq, hd = qkv.shape
    nh = N_HEADS_NA
    ncase, nk, nq = bias.shape[1:]

    def spec(off):
        return pl.BlockSpec((None, None, seq, hd), lambda b, h: (b, off + h, 0, 0))

    return pl.pallas_call(
        functools.partial(_na_kernel, offsets=offsets),
        grid=(bsz, nh),
        in_specs=[spec(0), spec(nh), spec(2 * nh),
                  pl.BlockSpec((None, ncase, nk, nq), lambda b, h: (h, 0, 0, 0))],
        out_specs=pl.BlockSpec((None, None, seq, hd), lambda b, h: (b, h, 0, 0)),
        out_shape=jax.ShapeDtypeStruct((bsz, nh, seq, hd), BF16),
        scratch_shapes=[pltpu.VMEM((nk, nq), F32), pltpu.VMEM((nk, nq), F32),
                        pltpu.VMEM((nk, nq), BF16), pltpu.VMEM((nk, nq), BF16),
                        pltpu.VMEM((8, nq), F32), pltpu.VMEM((8, nq), F32)],
        compiler_params=_cparams(("arbitrary", "arbitrary")),
        name="neighborhood_attention",
    )(qkv, qkv, qkv, bias)


def _out_kernel(a_ref, b_ref, x_ref, mod_ref, w_ref, o_ref, mix_s):
    na = a_ref.shape[0]
    for h in range(na):
        mix_s[:, h * HEAD_DIM:(h + 1) * HEAD_DIM] = a_ref[h]
    for h in range(b_ref.shape[0]):
        mix_s[:, (na + h) * HEAD_DIM:(na + h + 1) * HEAD_DIM] = b_ref[h]
    y = jnp.dot(mix_s[...], w_ref[...], preferred_element_type=F32)
    o_ref[...] = x_ref[...] + mod_ref[2:3, :] * y


def _out_proj(att_a, att_b, x, mod, w, tm=512):
    bsz, seq, d = x.shape
    na, nb = att_a.shape[1], att_b.shape[1]
    return pl.pallas_call(
        _out_kernel,
        grid=(bsz, seq // tm),
        in_specs=[
            pl.BlockSpec((None, na, tm, HEAD_DIM), lambda b, i: (b, 0, i, 0)),
            pl.BlockSpec((None, nb, tm, HEAD_DIM), lambda b, i: (b, 0, i, 0)),
            pl.BlockSpec((None, tm, d), lambda b, i: (b, i, 0)),
            pl.BlockSpec((None, 6, d), lambda b, i: (b, 0, 0)),
            pl.BlockSpec((d, d), lambda b, i: (0, 0), pipeline_mode=pl.Buffered(1)),
        ],
        out_specs=pl.BlockSpec((None, tm, d), lambda b, i: (b, i, 0)),
        out_shape=jax.ShapeDtypeStruct((bsz, seq, d), F32),
        scratch_shapes=[pltpu.VMEM((tm, d), BF16)],
        compiler_params=_cparams(("arbitrary", "arbitrary")),
        name="attn_out_proj",
    )(att_a, att_b, x, mod, w)


def _mlp_in_kernel(x_ref, mod_ref, ln_ref, w_ref, o_ref, h_s):
    @pl.when(pl.program_id(2) == 0)
    def _():
        _norm_modulate(x_ref, ln_ref, mod_ref[4:5, :], mod_ref[3:4, :], h_s)

    y = jnp.maximum(jnp.dot(h_s[...], w_ref[...], preferred_element_type=F32), 0.0)
    o_ref[...] = (y * y).astype(BF16)


def _mlp_in(x, mod, ln, w, tm=1024, tn=1024):
    bsz, seq, d = x.shape
    f = w.shape[1]
    return pl.pallas_call(
        _mlp_in_kernel,
        grid=(bsz, seq // tm, f // tn),
        in_specs=[
            pl.BlockSpec((None, tm, d), lambda b, i, j: (b, i, 0)),
            pl.BlockSpec((None, 6, d), lambda b, i, j: (b, 0, 0)),
            pl.BlockSpec((1, d), lambda b, i, j: (0, 0)),
            pl.BlockSpec((d, tn), lambda b, i, j: (0, j)),
        ],
        out_specs=pl.BlockSpec((None, tm, tn), lambda b, i, j: (b, i, j)),
        out_shape=jax.ShapeDtypeStruct((bsz, seq, f), BF16),
        scratch_shapes=[pltpu.VMEM((tm, d), BF16)],
        compiler_params=_cparams(("arbitrary", "arbitrary", "arbitrary")),
        name="mlp_in",
    )(x, mod, ln, w)


def _mlp_out_kernel(h_ref, x_ref, mod_ref, w_ref, o_ref):
    y = jnp.dot(h_ref[...], w_ref[...], preferred_element_type=F32)
    o_ref[...] = x_ref[...] + mod_ref[5:6, :] * y


def _mlp_out(hid, x, mod, w, tm=512, tn=512):
    bsz, seq, d = x.shape
    f = hid.shape[2]
    return pl.pallas_call(
        _mlp_out_kernel,
        grid=(bsz, seq // tm, d // tn),
        in_specs=[
            pl.BlockSpec((None, tm, f), lambda b, i, j: (b, i, 0)),
            pl.BlockSpec((None, tm, tn), lambda b, i, j: (b, i, j)),
            pl.BlockSpec((None, 6, tn), lambda b, i, j: (b, 0, j)),
            pl.BlockSpec((f, tn), lambda b, i, j: (0, j)),
        ],
        out_specs=pl.BlockSpec((None, tm, tn), lambda b, i, j: (b, i, j)),
        out_shape=jax.ShapeDtypeStruct((bsz, seq, d), F32),
        compiler_params=_cparams(("arbitrary", "arbitrary", "arbitrary")),
        name="mlp_out",
    )(hid, x, mod, w)


def _forward(x, c, ln1, w_ada, b_ada, w_in, q_norm_dil, k_norm_dil, q_norm_na, k_norm_na,
             na_rel_bias, w_out, ln2, w_mlp_in, w_mlp_out, dil_tq):
    bsz, seq, d = x.shape
    depth = w_ada.shape[0]
    rows = seq // GRID_W
    tables = _rope_tables(seq, _QKV_TM)
    mod_all = _modulation(c, w_ada, b_ada).reshape(depth, bsz, 6, d)
    for layer in range(depth):
        mod = mod_all[layer]
        gains = jnp.stack([q_norm_dil[layer], k_norm_dil[layer],
                           q_norm_na[layer], k_norm_na[layer]]).astype(F32)
        qkv_dil, qkv_na = _qkv_proj(x, mod, ln1[layer][None, :], w_in[layer].astype(BF16),
                                    gains, tables)
        att_dil = _dilated_attention(qkv_dil, tq=dil_tq)
        att_na = _neighborhood_attention(qkv_na, _na_bias_tables(na_rel_bias[layer], rows),
                                         _na_offsets(rows))
        x = _out_proj(att_dil, att_na, x, mod, w_out[layer].astype(BF16))
        hid = _mlp_in(x, mod, ln2[layer][None, :], w_mlp_in[layer].astype(BF16))
        x = _mlp_out(hid, x, mod, w_mlp_out[layer].astype(BF16))
    return x


def kernel(x, c, ln1, w_ada, b_ada, w_in, q_norm_dil, k_norm_dil, q_norm_na, k_norm_na,
           na_rel_bias, w_out, ln2, w_mlp_in, w_mlp_out):
    return _forward(x, c, ln1, w_ada, b_ada, w_in, q_norm_dil, k_norm_dil, q_norm_na,
                    k_norm_na, na_rel_bias, w_out, ln2, w_mlp_in, w_mlp_out, dil_tq=256)
```

```python
import functools

import numpy as np
import jax
import jax.numpy as jnp
from jax import lax
from jax.experimental import pallas as pl
from jax.experimental.pallas import tpu as pltpu

HEAD_DIM = 128
N_HEADS_DIL = 8
N_HEADS_NA = 8
ROT_DIM = HEAD_DIM // 4
ROPE_THETA = 500000.0
DIL_HALF = 64
DILATIONS = (16, 4, 1)
GRID_W = 64
NA_ROWS = 8
NA_COLS = 16
EPS = 1e-6
NEG_INF = -1e30
LOG2E = float(np.log2(np.e))

V7X_VMEM_BYTES = 64 * 1024 * 1024
VMEM_LIMIT = 56 * 1024 * 1024
DIL_VMEM_LIMIT = 60 * 1024 * 1024

F32 = jnp.float32
BF16 = jnp.bfloat16


def _cparams(sem):
    return pltpu.CompilerParams(dimension_semantics=sem, vmem_limit_bytes=VMEM_LIMIT)


def _mod_kernel(ct_ref, w_ref, b_ref, o_ref, act_s, *, nb):
    d = w_ref.shape[0]
    tn = w_ref.shape[1]

    @pl.when((pl.program_id(0) == 0) & (pl.program_id(1) == 0))
    def _():
        ct = ct_ref[...]
        act = ct * (1.0 / (1.0 + jnp.exp(-ct)))
        for b in range(nb):
            act_s[b] = jnp.broadcast_to(act[:, b:b + 1], (d, HEAD_DIM))

    def body(k, accs):
        r0 = pl.multiple_of(k * 8, 8)
        w8 = w_ref[pl.ds(r0, 8), :]
        out = []
        for b, acc in enumerate(accs):
            a = act_s[b, pl.ds(r0, 8), :]
            out.append(acc + w8 * jnp.concatenate([a] * (tn // HEAD_DIM), axis=1))
        return tuple(out)

    accs = lax.fori_loop(0, d // 8, body, tuple(jnp.zeros((8, tn), F32) for _ in range(nb)),
                         unroll=4)
    for b in range(nb):
        o_ref[b:b + 1, :] = jnp.sum(accs[b], axis=0, keepdims=True) + b_ref[...]


def _modulation(c, w_ada, b_ada):
    nl, d, n6 = w_ada.shape
    nb = c.shape[0]
    tn = 1024
    return pl.pallas_call(
        functools.partial(_mod_kernel, nb=nb),
        grid=(nl, n6 // tn),
        in_specs=[
            pl.BlockSpec((d, nb), lambda l, j: (0, 0)),
            pl.BlockSpec((None, d, tn), lambda l, j: (l, 0, j)),
            pl.BlockSpec((None, 1, tn), lambda l, j: (l, 0, j)),
        ],
        out_specs=pl.BlockSpec((None, nb, tn), lambda l, j: (l, 0, j)),
        out_shape=jax.ShapeDtypeStruct((nl, nb, n6), F32),
        scratch_shapes=[pltpu.VMEM((nb, d, HEAD_DIM), F32)],
        compiler_params=_cparams(("arbitrary", "arbitrary")),
        name="adaln_modulation",
    )(c.T, w_ada, b_ada.reshape(nl, 1, n6))


_NORM_CHUNK = 64


def _norm_modulate(x_ref, ln_ref, sc, sh, h_s):
    tm = x_ref.shape[0]
    gain = ln_ref[...] * (1.0 + sc)

    def body(c, carry):
        r0 = pl.multiple_of(c * _NORM_CHUNK, _NORM_CHUNK)
        x = x_ref[pl.ds(r0, _NORM_CHUNK), :]
        ms = jnp.mean(x * x, axis=-1, keepdims=True)
        h_s[pl.ds(r0, _NORM_CHUNK), :] = (x * lax.rsqrt(ms + EPS) * gain + sh).astype(BF16)
        return carry

    lax.fori_loop(0, tm // _NORM_CHUNK, body, 0)


def _qkv_kernel(x_ref, mod_ref, ln_ref, perm_ref, w_ref, gain_ref, cos_ref, sa_ref, sb_ref,
                oc_ref, on_ref, hn_s, hc_s, acc_s):
    scale = HEAD_DIM ** -0.5 * LOG2E
    nh = N_HEADS_DIL
    tn = nh * HEAD_DIM
    tm = x_ref.shape[0]
    ncls = oc_ref.shape[1]
    _norm_modulate(x_ref, ln_ref, mod_ref[1:2, :], mod_ref[0:1, :], hn_s)
    hc_s[...] = jnp.dot(perm_ref[...], hn_s[...], preferred_element_type=F32).astype(BF16)

    for g in range(w_ref.shape[1] // tn):
        rope = g < 3
        h_s = hc_s if rope else hn_s
        acc = acc_s.at[g % 2]
        acc[...] = jnp.dot(h_s[...], w_ref[:, g * tn:(g + 1) * tn], preferred_element_type=F32)
        kind = g % 3
        if kind < 2:
            row = 2 * (g // 3) + kind
            gain = gain_ref[row:row + 1, :] * (scale if kind == 0 else 1.0)
        for h in range(nh):
            y = acc[:, h * HEAD_DIM:(h + 1) * HEAD_DIM]
            if kind < 2:
                ms = jnp.mean(y * y, axis=-1, keepdims=True)
                y = y * lax.rsqrt(ms + EPS) * gain
            if rope and kind < 2:
                y = (y * cos_ref[...]
                     + pltpu.roll(y, ROT_DIM // 2, 1) * sa_ref[...]
                     + pltpu.roll(y, HEAD_DIM - ROT_DIM // 2, 1) * sb_ref[...])
            if rope:
                oc_ref[kind * nh + h] = y.astype(BF16).reshape(ncls, tm // ncls, HEAD_DIM)
            else:
                on_ref[kind * nh + h] = y.astype(BF16)


def _class_major_positions(seq, tm):
    dmax = DILATIONS[0]
    pos = np.arange(seq).reshape(seq // tm, tm // dmax, dmax)
    return pos.transpose(0, 2, 1).reshape(seq)


def _rope_tables(seq, tm):
    half = ROT_DIM // 2
    pos = _class_major_positions(seq, tm).astype(np.float64)[:, None]
    inv = ROPE_THETA ** (-np.arange(0, ROT_DIM, 2, dtype=np.float64) / ROT_DIM)[None, :]
    ang = pos * inv
    cos = np.ones((seq, HEAD_DIM), np.float64)
    sa = np.zeros((seq, HEAD_DIM), np.float64)
    sb = np.zeros((seq, HEAD_DIM), np.float64)
    cos[:, :half] = np.cos(ang)
    cos[:, half:ROT_DIM] = np.cos(ang)
    sa[:, half:ROT_DIM] = np.sin(ang)
    sb[:, :half] = -np.sin(ang)
    return (jnp.asarray(cos, F32), jnp.asarray(sa, F32), jnp.asarray(sb, F32))


_QKV_TM = 256


def _qkv_proj(x, mod, ln, w, layer, gains, tables, tm=_QKV_TM):
    bsz, seq, d = x.shape
    n = w.shape[2]
    tn = N_HEADS_DIL * HEAD_DIM
    nslab = n // HEAD_DIM // 2
    dmax = DILATIONS[0]
    cos, sa, sb = tables
    perm = _class_major_positions(tm, tm)
    perm_mat = jnp.asarray(np.eye(tm)[perm], BF16)
    tab_spec = pl.BlockSpec((tm, HEAD_DIM), lambda b, t: (t, 0))
    qkv_dil, qkv_na = pl.pallas_call(
        _qkv_kernel,
        grid=(bsz, seq // tm),
        in_specs=[
            pl.BlockSpec((None, tm, d), lambda b, t: (b, t, 0)),
            pl.BlockSpec((None, 6, d), lambda b, t: (b, 0, 0)),
            pl.BlockSpec((1, d), lambda b, t: (0, 0)),
            pl.BlockSpec((tm, tm), lambda b, t: (0, 0)),
            pl.BlockSpec((None, d, n), lambda b, t: (layer, 0, 0), pipeline_mode=pl.Buffered(1)),
            pl.BlockSpec((4, HEAD_DIM), lambda b, t: (0, 0)),
            tab_spec, tab_spec, tab_spec,
        ],
        out_specs=[
            pl.BlockSpec((None, nslab, dmax, tm // dmax, HEAD_DIM), lambda b, t: (b, 0, 0, t, 0)),
            pl.BlockSpec((None, nslab, tm, HEAD_DIM), lambda b, t: (b, 0, t, 0)),
        ],
        out_shape=[jax.ShapeDtypeStruct((bsz, nslab, dmax, seq // dmax, HEAD_DIM), BF16),
                   jax.ShapeDtypeStruct((bsz, nslab, seq, HEAD_DIM), BF16)],
        scratch_shapes=[pltpu.VMEM((tm, d), BF16), pltpu.VMEM((tm, d), BF16),
                        pltpu.VMEM((2, tm, tn), F32)],
        compiler_params=_cparams(("arbitrary", "arbitrary")),
        name="qkv_proj",
    )(x, mod, ln, perm_mat, w, gains, cos, sa, sb)
    return qkv_dil, qkv_na


_CAST_CHUNK = 256


def _dil_mask_tables(tq):
    kw = tq + 2 * DIL_HALF
    i = np.arange(tq)[:, None]
    j = np.arange(kw)[None, :]
    tabs = [np.where(np.abs(j - i - DIL_HALF * case) <= DIL_HALF, 0.0, NEG_INF) for case in range(3)]
    return jnp.asarray(np.stack(tabs), F32)


def _dil_kernel(q_ref, k_ref, v_ref, mask_ref, o_ref, a0, a1, a2, b0, b1, b2, c0, c1, c2,
                s0, s1, p0, p1, vb0, vb1, mb0, mb1, lb0, lb1, *, tq):
    seq = o_ref.shape[0]
    kw = tq + 2 * DIL_HALF
    s_buf, p_buf, v_buf, m_buf, l_buf = (s0, s1), (p0, p1), (vb0, vb1), (mb0, mb1), (lb0, lb1)
    set_a, set_b, set_c = (a0, a1, a2), (b0, b1, b2), (c0, c1, c2)
    step = DILATIONS[0] // DILATIONS[1]

    def finer_rows(dil, r, n0, size):
        fine = dil // step
        return pl.ds((r % fine) * (seq // fine) + r // fine + step * n0, size, stride=step)

    def class_rows(ref, cls_len, r, start, size, align):
        if len(ref.shape) == 3:
            return ref[r, pl.ds(pl.multiple_of(start, align), size), :]
        return ref[pl.ds(pl.multiple_of(r * cls_len + start, align), size), :]

    def reorder(dil, src, dst):
        cls_len = seq // dil
        nchunk = cls_len // _CAST_CHUNK

        def body(it, carry):
            r = it // nchunk
            n0 = (it % nchunk) * _CAST_CHUNK
            for s_ref, d_ref in zip(src, dst):
                chunk = class_rows(s_ref, cls_len, r, n0, _CAST_CHUNK, _CAST_CHUNK)
                d_ref[finer_rows(dil, r, n0, _CAST_CHUNK), :] = chunk.astype(F32)
            return carry

        lax.fori_loop(0, dil * nchunk, body, 0)

    def run_pattern(dil, mode, src, acc_in, acc_out):
        cls_len = seq // dil
        nblk = cls_len // tq
        total = dil * nblk
        qs, ks, vs = src

        def geom(it):
            it = jnp.minimum(it, total - 1)
            r = it // nblk
            n0 = (it % nblk) * tq
            nk0 = jnp.clip(n0 - DIL_HALF, 0, cls_len - kw)
            qrows = pl.ds(pl.multiple_of(r * cls_len + n0, tq), tq)
            return qrows, nk0, (n0 - nk0) // DIL_HALF, r, n0

        def scores(it, slot):
            _, nk0, case, r, n0 = geom(it)
            q = class_rows(qs, cls_len, r, n0, tq, tq).astype(BF16)
            k = class_rows(ks, cls_len, r, nk0, kw, DIL_HALF).astype(BF16)
            v_buf[slot][...] = class_rows(vs, cls_len, r, nk0, kw, DIL_HALF).astype(BF16)
            s = lax.dot_general(q, k, (((1,), (1,)), ((), ())), preferred_element_type=F32)
            s_buf[slot][...] = s + mask_ref[case]

        def softmax(slot):
            s = s_buf[slot][...]
            m = jnp.max(s, axis=-1, keepdims=True)
            p = jnp.exp2(s - m)
            l = jnp.sum(p, axis=-1, keepdims=True)
            p_buf[slot][...] = p.astype(BF16)
            m_buf[slot][...] = jnp.broadcast_to(m, (tq, HEAD_DIM))
            l_buf[slot][...] = jnp.broadcast_to(l, (tq, HEAD_DIM))

        def values(it, slot):
            qrows, _, _, r, n0 = geom(it)
            o = jnp.dot(p_buf[slot][...], v_buf[slot][...], preferred_element_type=F32)
            mb = m_buf[slot][...]
            lb = l_buf[slot][...]
            if mode != "init":
                acc_i, m_i, l_i = acc_in
                m_old = m_i[qrows, :]
                m_new = jnp.maximum(m_old, mb)
                a_old = jnp.exp2(m_old - m_new)
                a_new = jnp.exp2(mb - m_new)
                o = acc_i[qrows, :] * a_old + o * a_new
                lb = l_i[qrows, :] * a_old + lb * a_new
                mb = m_new
            if mode == "final":
                o_ref[qrows, :] = (o / lb).astype(BF16)
            else:
                acc_o, m_o, l_o = acc_out
                orows = finer_rows(dil, r, n0, tq)
                acc_o[orows, :] = o
                m_o[orows, :] = mb
                l_o[orows, :] = lb

        scores(0, 0)
        scores(1, 1)
        softmax(0)

        def body(j, carry):
            w = 2 * j
            values(w, 0)
            scores(w + 2, 0)
            softmax(1)
            values(w + 1, 1)
            scores(w + 3, 1)
            softmax(0)
            return carry

        lax.fori_loop(0, total // 2, body, 0)

    d16, d4, d1 = DILATIONS
    run_pattern(d16, "init", (q_ref, k_ref, v_ref), None, set_a)
    reorder(d16, (q_ref, k_ref, v_ref), set_b)
    run_pattern(d4, "merge", set_b, set_a, set_c)
    reorder(d4, set_b, set_a)
    run_pattern(d1, "final", set_a, set_c, None)


def _dilated_attention(qkv, tq=256):
    bsz, _, ncls, cls_len, hd = qkv.shape
    seq = ncls * cls_len
    nh = N_HEADS_DIL
    kw = tq + 2 * DIL_HALF
    masks = _dil_mask_tables(tq)

    def spec(off):
        return pl.BlockSpec((None, None, ncls, cls_len, hd), lambda b, h: (b, off + h, 0, 0, 0))

    def two(shape, dtype):
        return [pltpu.VMEM(shape, dtype), pltpu.VMEM(shape, dtype)]

    return pl.pallas_call(
        functools.partial(_dil_kernel, tq=tq),
        grid=(bsz, nh),
        in_specs=[spec(0), spec(nh), spec(2 * nh),
                  pl.BlockSpec(masks.shape, lambda b, h: (0, 0, 0), pipeline_mode=pl.Buffered(1))],
        out_specs=pl.BlockSpec((None, None, seq, hd), lambda b, h: (b, h, 0, 0)),
        out_shape=jax.ShapeDtypeStruct((bsz, nh, seq, hd), BF16),
        scratch_shapes=([pltpu.VMEM((seq, hd), F32) for _ in range(9)]
                        + two((tq, kw), F32) + two((tq, kw), BF16) + two((kw, hd), BF16)
                        + two((tq, hd), F32) + two((tq, hd), F32)),
        compiler_params=pltpu.CompilerParams(dimension_semantics=("arbitrary", "arbitrary"),
                                             vmem_limit_bytes=DIL_VMEM_LIMIT),
        name="dilated_attention",
    )(qkv, qkv, qkv, masks)


_NA_QROWS = 4
_NA_KROWS = NA_ROWS + _NA_QROWS - 1
_NA_CHUNK = 64


def _na_key_start(i0, rows):
    return np.clip(i0 - NA_ROWS // 2, 0, rows - _NA_KROWS)


def _na_offsets(rows):
    return tuple(sorted({int(i0 - _na_key_start(i0, rows)) for i0 in range(0, rows, _NA_QROWS)}))


def _na_bias_tables(rpb, rows):
    nq = _NA_QROWS * GRID_W
    nk = _NA_KROWS * GRID_W
    nh, nrow, ncol = rpb.shape
    w = GRID_W
    u = jnp.pad(rpb.astype(F32) * LOG2E, ((0, 0), (0, 0), (w - NA_COLS, w + NA_COLS - ncol)))
    toep = jnp.broadcast_to(u[:, :, None, :], (nh, nrow, w, 2 * w)).reshape(nh, nrow, 2 * w * w)
    toep = toep[:, :, :w * (2 * w - 1)].reshape(nh, nrow, w, 2 * w - 1)[:, :, :, w - 1:]
    cs = np.clip(np.arange(w) - NA_COLS // 2, 0, w - NA_COLS)
    cvalid = (np.arange(w)[None, :] >= cs[:, None]) & (np.arange(w)[None, :] < cs[:, None] + NA_COLS)
    def row_pattern(i0):
        ks = int(_na_key_start(i0, rows))
        roff, rvalid = [], []
        for qi in range(i0, i0 + _NA_QROWS):
            rs = int(np.clip(qi - NA_ROWS // 2, 0, rows - NA_ROWS))
            for kr in range(ks, ks + _NA_KROWS):
                roff.append(int(np.clip(kr - qi + NA_ROWS - 1, 0, nrow - 1)))
                rvalid.append(rs <= kr < rs + NA_ROWS)
        return roff, rvalid

    slabs, rvalids = [], []
    for off in _na_offsets(rows):
        blocks = [i0 for i0 in range(0, rows, _NA_QROWS) if i0 - _na_key_start(i0, rows) == off]
        roff, rvalid = row_pattern(blocks[0])
        assert all(row_pattern(i0) == (roff, rvalid) for i0 in blocks)
        slabs += [toep[:, r] for r in roff]
        rvalids += rvalid
    ncase = len(_na_offsets(rows))
    tab = jnp.stack(slabs, axis=1).reshape(nh, ncase, _NA_QROWS, _NA_KROWS, w, w)
    rvalid = np.asarray(rvalids).reshape(ncase, _NA_QROWS, _NA_KROWS)
    valid = rvalid[:, :, :, None, None] & cvalid[None, None, None]
    tab = jnp.where(valid[None], tab, NEG_INF)
    return tab.transpose(0, 1, 3, 5, 2, 4).reshape(nh, ncase, nk, nq)


def _na_kernel(q_ref, k_ref, v_ref, bias_ref, o_ref, s0, s1, p0, p1, l0, l1, *, offsets):
    seq = q_ref.shape[0]
    rows = seq // GRID_W
    nq = _NA_QROWS * GRID_W
    nk = _NA_KROWS * GRID_W
    nblk = rows // _NA_QROWS
    s_buf, p_buf, l_buf = (s0, s1), (p0, p1), (l0, l1)

    def geom(blk):
        blk = jnp.minimum(blk, nblk - 1)
        i0 = blk * _NA_QROWS
        ks = jnp.clip(i0 - NA_ROWS // 2, 0, rows - _NA_KROWS)
        case = sum((i0 - ks >= off).astype(jnp.int32) for off in offsets[1:])
        qrows = pl.ds(pl.multiple_of(i0 * GRID_W, nq), nq)
        krows = pl.ds(pl.multiple_of(ks * GRID_W, GRID_W), nk)
        return qrows, krows, case

    def scores(blk, slot):
        qrows, krows, case = geom(blk)
        st = lax.dot_general(k_ref[krows, :], q_ref[qrows, :], (((1,), (1,)), ((), ())),
                             preferred_element_type=F32)
        s_buf[slot][...] = st + bias_ref[case]

    def softmax(slot):
        s = s_buf[slot]
        nchunk = nk // _NA_CHUNK
        m = s[0:_NA_CHUNK, :]
        for c in range(1, nchunk):
            m = jnp.maximum(m, s[c * _NA_CHUNK:(c + 1) * _NA_CHUNK, :])
        m = jnp.max(m, axis=0, keepdims=True)
        l = jnp.zeros((_NA_CHUNK, nq), F32)
        for c in range(nchunk):
            p = jnp.exp2(s[c * _NA_CHUNK:(c + 1) * _NA_CHUNK, :] - m)
            l = l + p
            p_buf[slot][c * _NA_CHUNK:(c + 1) * _NA_CHUNK, :] = p.astype(BF16)
        l_buf[slot][...] = jnp.broadcast_to(jnp.sum(l, axis=0, keepdims=True), l_buf[slot].shape)

    def values(blk, slot):
        qrows, krows, _ = geom(blk)
        ot = lax.dot_general(v_ref[krows, :], p_buf[slot][...], (((0,), (0,)), ((), ())),
                             preferred_element_type=F32)
        ot = ot * (1.0 / l_buf[slot][0:1, :])
        o_ref[qrows, :] = ot.T.astype(BF16)

    scores(0, 0)
    scores(1, 1)
    softmax(0)

    def body(j, carry):
        w = 2 * j
        scores(w + 2, 0)
        softmax(1)
        values(w, 0)
        scores(w + 3, 1)
        softmax(0)
        values(w + 1, 1)
        return carry

    lax.fori_loop(0, nblk // 2, body, 0)


def _neighborhood_attention(qkv, bias, offsets):
    bsz, _, seq, hd = qkv.shape
    nh = N_HEADS_NA
    ncase, nk, nq = bias.shape[1:]

    def spec(off):
        return pl.BlockSpec((None, None, seq, hd), lambda b, h: (b, off + h, 0, 0))

    return pl.pallas_call(
        functools.partial(_na_kernel, offsets=offsets),
        grid=(bsz, nh),
        in_specs=[spec(0), spec(nh), spec(2 * nh),
                  pl.BlockSpec((None, ncase, nk, nq), lambda b, h: (h, 0, 0, 0))],
        out_specs=pl.BlockSpec((None, None, seq, hd), lambda b, h: (b, h, 0, 0)),
        out_shape=jax.ShapeDtypeStruct((bsz, nh, seq, hd), BF16),
        scratch_shapes=[pltpu.VMEM((nk, nq), F32), pltpu.VMEM((nk, nq), F32),
                        pltpu.VMEM((nk, nq), BF16), pltpu.VMEM((nk, nq), BF16),
                        pltpu.VMEM((16, nq), F32), pltpu.VMEM((16, nq), F32)],
        compiler_params=_cparams(("arbitrary", "arbitrary")),
        name="neighborhood_attention",
    )(qkv, qkv, qkv, bias)


def _out_kernel(a_ref, b_ref, x_ref, mod_ref, w_ref, o_ref, mix_s):
    na = a_ref.shape[0]
    for h in range(na):
        mix_s[:, h * HEAD_DIM:(h + 1) * HEAD_DIM] = a_ref[h]
    for h in range(b_ref.shape[0]):
        mix_s[:, (na + h) * HEAD_DIM:(na + h + 1) * HEAD_DIM] = b_ref[h]
    y = jnp.dot(mix_s[...], w_ref[...], preferred_element_type=F32)
    o_ref[...] = x_ref[...] + mod_ref[2:3, :] * y


def _out_proj(att_a, att_b, x, mod, w, layer, tm=512):
    bsz, seq, d = x.shape
    na, nb = att_a.shape[1], att_b.shape[1]
    return pl.pallas_call(
        _out_kernel,
        grid=(bsz, seq // tm),
        in_specs=[
            pl.BlockSpec((None, na, tm, HEAD_DIM), lambda b, i: (b, 0, i, 0)),
            pl.BlockSpec((None, nb, tm, HEAD_DIM), lambda b, i: (b, 0, i, 0)),
            pl.BlockSpec((None, tm, d), lambda b, i: (b, i, 0)),
            pl.BlockSpec((None, 6, d), lambda b, i: (b, 0, 0)),
            pl.BlockSpec((None, d, d), lambda b, i: (layer, 0, 0), pipeline_mode=pl.Buffered(1)),
        ],
        out_specs=pl.BlockSpec((None, tm, d), lambda b, i: (b, i, 0)),
        out_shape=jax.ShapeDtypeStruct((bsz, seq, d), F32),
        scratch_shapes=[pltpu.VMEM((tm, d), BF16)],
        compiler_params=_cparams(("arbitrary", "arbitrary")),
        name="attn_out_proj",
    )(att_a, att_b, x, mod, w)


def _mlp_in_kernel(x_ref, mod_ref, ln_ref, w_ref, o_ref, h_s):
    @pl.when(pl.program_id(2) == 0)
    def _():
        _norm_modulate(x_ref, ln_ref, mod_ref[4:5, :], mod_ref[3:4, :], h_s)

    y = jnp.maximum(jnp.dot(h_s[...], w_ref[...], preferred_element_type=F32), 0.0)
    o_ref[...] = (y * y).astype(BF16)


def _mlp_in(x, mod, ln, w, layer, tm=1024, tn=1024):
    bsz, seq, d = x.shape
    f = w.shape[2]
    return pl.pallas_call(
        _mlp_in_kernel,
        grid=(bsz, seq // tm, f // tn),
        in_specs=[
            pl.BlockSpec((None, tm, d), lambda b, i, j: (b, i, 0)),
            pl.BlockSpec((None, 6, d), lambda b, i, j: (b, 0, 0)),
            pl.BlockSpec((1, d), lambda b, i, j: (0, 0)),
            pl.BlockSpec((None, d, tn), lambda b, i, j: (layer, 0, j)),
        ],
        out_specs=pl.BlockSpec((None, tm, tn), lambda b, i, j: (b, i, j)),
        out_shape=jax.ShapeDtypeStruct((bsz, seq, f), BF16),
        scratch_shapes=[pltpu.VMEM((tm, d), BF16)],
        compiler_params=_cparams(("arbitrary", "arbitrary", "arbitrary")),
        name="mlp_in",
    )(x, mod, ln, w)


def _mlp_out_kernel(h_ref, x_ref, mod_ref, w_ref, o_ref):
    y = jnp.dot(h_ref[...], w_ref[...], preferred_element_type=F32)
    o_ref[...] = x_ref[...] + mod_ref[5:6, :] * y


def _mlp_out(hid, x, mod, w, layer, tm=1024, tn=256):
    bsz, seq, d = x.shape
    f = hid.shape[2]
    return pl.pallas_call(
        _mlp_out_kernel,
        grid=(bsz, seq // tm, d // tn),
        in_specs=[
            pl.BlockSpec((None, tm, f), lambda b, i, j: (b, i, 0)),
            pl.BlockSpec((None, tm, tn), lambda b, i, j: (b, i, j)),
            pl.BlockSpec((None, 6, tn), lambda b, i, j: (b, 0, j)),
            pl.BlockSpec((None, f, tn), lambda b, i, j: (layer, 0, j)),
        ],
        out_specs=pl.BlockSpec((None, tm, tn), lambda b, i, j: (b, i, j)),
        out_shape=jax.ShapeDtypeStruct((bsz, seq, d), F32),
        compiler_params=_cparams(("arbitrary", "arbitrary", "arbitrary")),
        name="mlp_out",
    )(hid, x, mod, w)


def _forward(x, c, ln1, w_ada, b_ada, w_in, q_norm_dil, k_norm_dil, q_norm_na, k_norm_na,
             na_rel_bias, w_out, ln2, w_mlp_in, w_mlp_out, dil_tq):
    bsz, seq, d = x.shape
    depth = w_ada.shape[0]
    rows = seq // GRID_W
    tables = _rope_tables(seq, _QKV_TM)
    mod_all = _modulation(c, w_ada, b_ada).reshape(depth, bsz, 6, d)
    w_in, w_out, w_mlp_in, w_mlp_out = (w.astype(BF16) for w in (w_in, w_out, w_mlp_in, w_mlp_out))
    for layer in range(depth):
        mod = mod_all[layer]
        gains = jnp.stack([q_norm_dil[layer], k_norm_dil[layer],
                           q_norm_na[layer], k_norm_na[layer]]).astype(F32)
        qkv_dil, qkv_na = _qkv_proj(x, mod, ln1[layer][None, :], w_in, layer, gains, tables)
        att_dil = _dilated_attention(qkv_dil, tq=dil_tq)
        att_na = _neighborhood_attention(qkv_na, _na_bias_tables(na_rel_bias[layer], rows),
                                         _na_offsets(rows))
        x = _out_proj(att_dil, att_na, x, mod, w_out, layer)
        hid = _mlp_in(x, mod, ln2[layer][None, :], w_mlp_in, layer)
        x = _mlp_out(hid, x, mod, w_mlp_out, layer)
    return x


def kernel(x, c, ln1, w_ada, b_ada, w_in, q_norm_dil, k_norm_dil, q_norm_na, k_norm_na,
           na_rel_bias, w_out, ln2, w_mlp_in, w_mlp_out):
    return _forward(x, c, ln1, w_ada, b_ada, w_in, q_norm_dil, k_norm_dil, q_norm_na,
                    k_norm_na, na_rel_bias, w_out, ln2, w_mlp_in, w_mlp_out, dil_tq=256)
```

```python
import functools

import numpy as np
import jax
import jax.numpy as jnp
from jax import lax
from jax.experimental import pallas as pl
from jax.experimental.pallas import tpu as pltpu

HEAD_DIM = 128
N_HEADS_DIL = 8
N_HEADS_NA = 8
ROT_DIM = HEAD_DIM // 4
ROPE_THETA = 500000.0
DIL_HALF = 64
DILATIONS = (16, 4, 1)
GRID_W = 64
NA_ROWS = 8
NA_COLS = 16
EPS = 1e-6
NEG_INF = -1e30
LOG2E = float(np.log2(np.e))

V7X_VMEM_BYTES = 64 * 1024 * 1024
VMEM_LIMIT = 56 * 1024 * 1024
BIG_VMEM_LIMIT = 60 * 1024 * 1024
DIL_VMEM_LIMIT = BIG_VMEM_LIMIT

F32 = jnp.float32
BF16 = jnp.bfloat16


def _cparams(sem):
    return pltpu.CompilerParams(dimension_semantics=sem, vmem_limit_bytes=VMEM_LIMIT)


def _mod_kernel(ct_ref, w_ref, b_ref, o_ref, act_s, *, nb):
    d = w_ref.shape[0]
    tn = w_ref.shape[1]

    @pl.when((pl.program_id(0) == 0) & (pl.program_id(1) == 0))
    def _():
        ct = ct_ref[...]
        act = ct * (1.0 / (1.0 + jnp.exp(-ct)))
        for b in range(nb):
            act_s[b] = jnp.broadcast_to(act[:, b:b + 1], (d, HEAD_DIM))

    def body(k, accs):
        r0 = pl.multiple_of(k * 8, 8)
        w8 = w_ref[pl.ds(r0, 8), :]
        out = []
        for b, acc in enumerate(accs):
            a = act_s[b, pl.ds(r0, 8), :]
            out.append(acc + w8 * jnp.concatenate([a] * (tn // HEAD_DIM), axis=1))
        return tuple(out)

    accs = lax.fori_loop(0, d // 8, body, tuple(jnp.zeros((8, tn), F32) for _ in range(nb)),
                         unroll=4)
    for b in range(nb):
        o_ref[b:b + 1, :] = jnp.sum(accs[b], axis=0, keepdims=True) + b_ref[...]


def _modulation(c, w_ada, b_ada):
    nl, d, n6 = w_ada.shape
    nb = c.shape[0]
    tn = 1024
    return pl.pallas_call(
        functools.partial(_mod_kernel, nb=nb),
        grid=(nl, n6 // tn),
        in_specs=[
            pl.BlockSpec((d, nb), lambda l, j: (0, 0)),
            pl.BlockSpec((None, d, tn), lambda l, j: (l, 0, j)),
            pl.BlockSpec((None, 1, tn), lambda l, j: (l, 0, j)),
        ],
        out_specs=pl.BlockSpec((None, nb, tn), lambda l, j: (l, 0, j)),
        out_shape=jax.ShapeDtypeStruct((nl, nb, n6), F32),
        scratch_shapes=[pltpu.VMEM((nb, d, HEAD_DIM), F32)],
        compiler_params=_cparams(("arbitrary", "arbitrary")),
        name="adaln_modulation",
    )(c.T, w_ada, b_ada.reshape(nl, 1, n6))


_NORM_CHUNK = 64


def _norm_modulate(x_ref, ln_ref, sc, sh, h_s):
    tm = x_ref.shape[0]
    gain = ln_ref[...] * (1.0 + sc)

    def body(c, carry):
        r0 = pl.multiple_of(c * _NORM_CHUNK, _NORM_CHUNK)
        x = x_ref[pl.ds(r0, _NORM_CHUNK), :]
        ms = jnp.mean(x * x, axis=-1, keepdims=True)
        h_s[pl.ds(r0, _NORM_CHUNK), :] = (x * lax.rsqrt(ms + EPS) * gain + sh).astype(BF16)
        return carry

    lax.fori_loop(0, tm // _NORM_CHUNK, body, 0)


def _qkv_kernel(x_ref, mod_ref, ln_ref, perm_ref, w_ref, gain_ref, cos_ref, sa_ref, sb_ref,
                oc_ref, on_ref, hn_s, hc_s, acc_s):
    scale = HEAD_DIM ** -0.5 * LOG2E
    nh = N_HEADS_DIL
    tn = nh * HEAD_DIM
    tm = x_ref.shape[0]
    ncls = oc_ref.shape[1]
    _norm_modulate(x_ref, ln_ref, mod_ref[1:2, :], mod_ref[0:1, :], hn_s)
    hc_s[...] = jnp.dot(perm_ref[...], hn_s[...], preferred_element_type=F32).astype(BF16)

    for g in range(w_ref.shape[1] // tn):
        rope = g < 3
        h_s = hc_s if rope else hn_s
        acc = acc_s.at[g % 2]
        acc[...] = jnp.dot(h_s[...], w_ref[:, g * tn:(g + 1) * tn], preferred_element_type=F32)
        kind = g % 3
        if kind < 2:
            row = 2 * (g // 3) + kind
            gain = gain_ref[row:row + 1, :] * (scale if kind == 0 else 1.0)
        for h in range(nh):
            y = acc[:, h * HEAD_DIM:(h + 1) * HEAD_DIM]
            if kind < 2:
                ms = jnp.mean(y * y, axis=-1, keepdims=True)
                y = y * lax.rsqrt(ms + EPS) * gain
            if rope and kind < 2:
                y = (y * cos_ref[...]
                     + pltpu.roll(y, ROT_DIM // 2, 1) * sa_ref[...]
                     + pltpu.roll(y, HEAD_DIM - ROT_DIM // 2, 1) * sb_ref[...])
            if rope:
                oc_ref[kind * nh + h] = y.astype(BF16).reshape(ncls, tm // ncls, HEAD_DIM)
            else:
                on_ref[kind * nh + h] = y.astype(BF16)


def _class_major_positions(seq, tm):
    dmax = DILATIONS[0]
    pos = np.arange(seq).reshape(seq // tm, tm // dmax, dmax)
    return pos.transpose(0, 2, 1).reshape(seq)


def _rope_tables(seq, tm):
    half = ROT_DIM // 2
    pos = _class_major_positions(seq, tm).astype(np.float64)[:, None]
    inv = ROPE_THETA ** (-np.arange(0, ROT_DIM, 2, dtype=np.float64) / ROT_DIM)[None, :]
    ang = pos * inv
    cos = np.ones((seq, HEAD_DIM), np.float64)
    sa = np.zeros((seq, HEAD_DIM), np.float64)
    sb = np.zeros((seq, HEAD_DIM), np.float64)
    cos[:, :half] = np.cos(ang)
    cos[:, half:ROT_DIM] = np.cos(ang)
    sa[:, half:ROT_DIM] = np.sin(ang)
    sb[:, :half] = -np.sin(ang)
    return (jnp.asarray(cos, F32), jnp.asarray(sa, F32), jnp.asarray(sb, F32))


_QKV_TM = 256


def _qkv_proj(x, mod, ln, w, layer, gains, tables, tm=_QKV_TM):
    bsz, seq, d = x.shape
    n = w.shape[2]
    tn = N_HEADS_DIL * HEAD_DIM
    nslab = n // HEAD_DIM // 2
    dmax = DILATIONS[0]
    cos, sa, sb = tables
    perm = _class_major_positions(tm, tm)
    perm_mat = jnp.asarray(np.eye(tm)[perm], BF16)
    tab_spec = pl.BlockSpec((tm, HEAD_DIM), lambda b, t: (t, 0))
    qkv_dil, qkv_na = pl.pallas_call(
        _qkv_kernel,
        grid=(bsz, seq // tm),
        in_specs=[
            pl.BlockSpec((None, tm, d), lambda b, t: (b, t, 0)),
            pl.BlockSpec((None, 6, d), lambda b, t: (b, 0, 0)),
            pl.BlockSpec((1, d), lambda b, t: (0, 0)),
            pl.BlockSpec((tm, tm), lambda b, t: (0, 0)),
            pl.BlockSpec((None, d, n), lambda b, t: (layer, 0, 0), pipeline_mode=pl.Buffered(1)),
            pl.BlockSpec((4, HEAD_DIM), lambda b, t: (0, 0)),
            tab_spec, tab_spec, tab_spec,
        ],
        out_specs=[
            pl.BlockSpec((None, nslab, dmax, tm // dmax, HEAD_DIM), lambda b, t: (b, 0, 0, t, 0)),
            pl.BlockSpec((None, nslab, tm, HEAD_DIM), lambda b, t: (b, 0, t, 0)),
        ],
        out_shape=[jax.ShapeDtypeStruct((bsz, nslab, dmax, seq // dmax, HEAD_DIM), BF16),
                   jax.ShapeDtypeStruct((bsz, nslab, seq, HEAD_DIM), BF16)],
        scratch_shapes=[pltpu.VMEM((tm, d), BF16), pltpu.VMEM((tm, d), BF16),
                        pltpu.VMEM((2, tm, tn), F32)],
        compiler_params=_cparams(("arbitrary", "arbitrary")),
        name="qkv_proj",
    )(x, mod, ln, perm_mat, w, gains, cos, sa, sb)
    return qkv_dil, qkv_na


_CAST_CHUNK = 256


def _dil_mask_tables(tq):
    kw = tq + 2 * DIL_HALF
    i = np.arange(tq)[:, None]
    j = np.arange(kw)[None, :]
    tabs = [np.where(np.abs(j - i - DIL_HALF * case) <= DIL_HALF, 0.0, NEG_INF) for case in range(3)]
    return jnp.asarray(np.stack(tabs), F32)


def _dil_kernel(q_ref, k_ref, v_ref, mask_ref, o_ref, a0, a1, a2, b0, b1, b2, c0, c1, c2,
                s0, s1, p0, p1, vb0, vb1, mb0, mb1, lb0, lb1, *, tq):
    seq = o_ref.shape[0]
    kw = tq + 2 * DIL_HALF
    s_buf, p_buf, v_buf, m_buf, l_buf = (s0, s1), (p0, p1), (vb0, vb1), (mb0, mb1), (lb0, lb1)
    set_a, set_b, set_c = (a0, a1, a2), (b0, b1, b2), (c0, c1, c2)
    step = DILATIONS[0] // DILATIONS[1]

    def finer_rows(dil, r, n0, size):
        fine = dil // step
        return pl.ds((r % fine) * (seq // fine) + r // fine + step * n0, size, stride=step)

    def class_rows(ref, cls_len, r, start, size, align):
        if len(ref.shape) == 3:
            return ref[r, pl.ds(pl.multiple_of(start, align), size), :]
        return ref[pl.ds(pl.multiple_of(r * cls_len + start, align), size), :]

    def reorder(dil, src, dst):
        cls_len = seq // dil
        nchunk = cls_len // _CAST_CHUNK

        def body(it, carry):
            r = it // nchunk
            n0 = (it % nchunk) * _CAST_CHUNK
            for s_ref, d_ref in zip(src, dst):
                chunk = class_rows(s_ref, cls_len, r, n0, _CAST_CHUNK, _CAST_CHUNK)
                d_ref[finer_rows(dil, r, n0, _CAST_CHUNK), :] = chunk.astype(F32)
            return carry

        lax.fori_loop(0, dil * nchunk, body, 0)

    def run_pattern(dil, mode, src, acc_in, acc_out):
        cls_len = seq // dil
        nblk = cls_len // tq
        total = dil * nblk
        qs, ks, vs = src

        def geom(it):
            it = jnp.minimum(it, total - 1)
            r = it // nblk
            n0 = (it % nblk) * tq
            nk0 = jnp.clip(n0 - DIL_HALF, 0, cls_len - kw)
            qrows = pl.ds(pl.multiple_of(r * cls_len + n0, tq), tq)
            return qrows, nk0, (n0 - nk0) // DIL_HALF, r, n0

        def scores(it, slot):
            _, nk0, case, r, n0 = geom(it)
            q = class_rows(qs, cls_len, r, n0, tq, tq).astype(BF16)
            k = class_rows(ks, cls_len, r, nk0, kw, DIL_HALF).astype(BF16)
            v_buf[slot][...] = class_rows(vs, cls_len, r, nk0, kw, DIL_HALF).astype(BF16)
            s = lax.dot_general(q, k, (((1,), (1,)), ((), ())), preferred_element_type=F32)
            s_buf[slot][...] = s + mask_ref[case]

        def softmax(slot):
            s = s_buf[slot][...]
            m = jnp.max(s, axis=-1, keepdims=True)
            p = jnp.exp2(s - m)
            l = jnp.sum(p, axis=-1, keepdims=True)
            p_buf[slot][...] = p.astype(BF16)
            m_buf[slot][...] = jnp.broadcast_to(m, (tq, HEAD_DIM))
            l_buf[slot][...] = jnp.broadcast_to(l, (tq, HEAD_DIM))

        def values(it, slot):
            qrows, _, _, r, n0 = geom(it)
            o = jnp.dot(p_buf[slot][...], v_buf[slot][...], preferred_element_type=F32)
            mb = m_buf[slot][...]
            lb = l_buf[slot][...]
            if mode != "init":
                acc_i, m_i, l_i = acc_in
                m_old = m_i[qrows, :]
                m_new = jnp.maximum(m_old, mb)
                a_old = jnp.exp2(m_old - m_new)
                a_new = jnp.exp2(mb - m_new)
                o = acc_i[qrows, :] * a_old + o * a_new
                lb = l_i[qrows, :] * a_old + lb * a_new
                mb = m_new
            if mode == "final":
                o_ref[qrows, :] = (o / lb).astype(BF16)
            else:
                acc_o, m_o, l_o = acc_out
                orows = finer_rows(dil, r, n0, tq)
                acc_o[orows, :] = o
                m_o[orows, :] = mb
                l_o[orows, :] = lb

        scores(0, 0)
        scores(1, 1)
        softmax(0)

        def body(j, carry):
            w = 2 * j
            values(w, 0)
            scores(w + 2, 0)
            softmax(1)
            values(w + 1, 1)
            scores(w + 3, 1)
            softmax(0)
            return carry

        lax.fori_loop(0, total // 2, body, 0)

    d16, d4, d1 = DILATIONS
    run_pattern(d16, "init", (q_ref, k_ref, v_ref), None, set_a)
    reorder(d16, (q_ref, k_ref, v_ref), set_b)
    run_pattern(d4, "merge", set_b, set_a, set_c)
    reorder(d4, set_b, set_a)
    run_pattern(d1, "final", set_a, set_c, None)


def _dilated_attention(qkv, tq=256):
    bsz, _, ncls, cls_len, hd = qkv.shape
    seq = ncls * cls_len
    nh = N_HEADS_DIL
    kw = tq + 2 * DIL_HALF
    masks = _dil_mask_tables(tq)

    def spec(off):
        return pl.BlockSpec((None, None, ncls, cls_len, hd), lambda b, h: (b, off + h, 0, 0, 0))

    def two(shape, dtype):
        return [pltpu.VMEM(shape, dtype), pltpu.VMEM(shape, dtype)]

    return pl.pallas_call(
        functools.partial(_dil_kernel, tq=tq),
        grid=(bsz, nh),
        in_specs=[spec(0), spec(nh), spec(2 * nh),
                  pl.BlockSpec(masks.shape, lambda b, h: (0, 0, 0), pipeline_mode=pl.Buffered(1))],
        out_specs=pl.BlockSpec((None, None, seq, hd), lambda b, h: (b, h, 0, 0)),
        out_shape=jax.ShapeDtypeStruct((bsz, nh, seq, hd), BF16),
        scratch_shapes=([pltpu.VMEM((seq, hd), F32) for _ in range(9)]
                        + two((tq, kw), F32) + two((tq, kw), BF16) + two((kw, hd), BF16)
                        + two((tq, hd), F32) + two((tq, hd), F32)),
        compiler_params=pltpu.CompilerParams(dimension_semantics=("arbitrary", "arbitrary"),
                                             vmem_limit_bytes=DIL_VMEM_LIMIT),
        name="dilated_attention",
    )(qkv, qkv, qkv, masks)


_NA_QROWS = 4
_NA_KROWS = NA_ROWS + _NA_QROWS - 1
_NA_CHUNK = 64


def _na_key_start(i0, rows):
    return np.clip(i0 - NA_ROWS // 2, 0, rows - _NA_KROWS)


def _na_offsets(rows):
    return tuple(sorted({int(i0 - _na_key_start(i0, rows)) for i0 in range(0, rows, _NA_QROWS)}))


def _na_bias_tables(rpb, rows):
    nq = _NA_QROWS * GRID_W
    nk = _NA_KROWS * GRID_W
    nh, nrow, ncol = rpb.shape
    w = GRID_W
    u = jnp.pad(rpb.astype(F32) * LOG2E, ((0, 0), (0, 0), (w - NA_COLS, w + NA_COLS - ncol)))
    toep = jnp.broadcast_to(u[:, :, None, :], (nh, nrow, w, 2 * w)).reshape(nh, nrow, 2 * w * w)
    toep = toep[:, :, :w * (2 * w - 1)].reshape(nh, nrow, w, 2 * w - 1)[:, :, :, w - 1:]
    cs = np.clip(np.arange(w) - NA_COLS // 2, 0, w - NA_COLS)
    cvalid = (np.arange(w)[None, :] >= cs[:, None]) & (np.arange(w)[None, :] < cs[:, None] + NA_COLS)
    toep = jnp.pad(toep, ((0, 0), (_NA_KROWS, _NA_KROWS), (0, 0), (0, 0)))

    def row_pattern(i0):
        ks = int(_na_key_start(i0, rows))
        first, rvalid = [], []
        for qi in range(i0, i0 + _NA_QROWS):
            rs = int(np.clip(qi - NA_ROWS // 2, 0, rows - NA_ROWS))
            first.append(ks - qi + NA_ROWS - 1 + _NA_KROWS)
            rvalid += [rs <= kr < rs + NA_ROWS for kr in range(ks, ks + _NA_KROWS)]
        return first, rvalid

    slabs, rvalids = [], []
    for off in _na_offsets(rows):
        blocks = [i0 for i0 in range(0, rows, _NA_QROWS) if i0 - _na_key_start(i0, rows) == off]
        first, rvalid = row_pattern(blocks[0])
        assert all(row_pattern(i0) == (first, rvalid) for i0 in blocks)
        slabs += [toep[:, lo:lo + _NA_KROWS] for lo in first]
        rvalids += rvalid
    ncase = len(_na_offsets(rows))
    tab = jnp.stack(slabs, axis=1).reshape(nh, ncase, _NA_QROWS, _NA_KROWS, w, w)
    rvalid = np.asarray(rvalids).reshape(ncase, _NA_QROWS, _NA_KROWS)
    valid = rvalid[:, :, :, None, None] & cvalid[None, None, None]
    tab = jnp.where(valid[None], tab, NEG_INF)
    return tab.transpose(0, 1, 3, 5, 2, 4).reshape(nh, ncase, nk, nq)


def _na_kernel(q_ref, k_ref, v_ref, bias_ref, o_ref, s0, s1, p0, p1, l0, l1, *, offsets):
    seq = q_ref.shape[0]
    rows = seq // GRID_W
    nq = _NA_QROWS * GRID_W
    nk = _NA_KROWS * GRID_W
    nblk = rows // _NA_QROWS
    s_buf, p_buf, l_buf = (s0, s1), (p0, p1), (l0, l1)

    def geom(blk):
        blk = jnp.minimum(blk, nblk - 1)
        i0 = blk * _NA_QROWS
        ks = jnp.clip(i0 - NA_ROWS // 2, 0, rows - _NA_KROWS)
        case = sum((i0 - ks >= off).astype(jnp.int32) for off in offsets[1:])
        qrows = pl.ds(pl.multiple_of(i0 * GRID_W, nq), nq)
        krows = pl.ds(pl.multiple_of(ks * GRID_W, GRID_W), nk)
        return qrows, krows, case

    def scores(blk, slot):
        qrows, krows, case = geom(blk)
        st = lax.dot_general(k_ref[krows, :], q_ref[qrows, :], (((1,), (1,)), ((), ())),
                             preferred_element_type=F32)
        s_buf[slot][...] = st + bias_ref[case]

    def softmax(slot):
        s = s_buf[slot]
        nchunk = nk // _NA_CHUNK
        m = s[0:_NA_CHUNK, :]
        for c in range(1, nchunk):
            m = jnp.maximum(m, s[c * _NA_CHUNK:(c + 1) * _NA_CHUNK, :])
        m = jnp.max(m, axis=0, keepdims=True)
        l = jnp.zeros((_NA_CHUNK, nq), F32)
        for c in range(nchunk):
            p = jnp.exp2(s[c * _NA_CHUNK:(c + 1) * _NA_CHUNK, :] - m)
            l = l + p
            p_buf[slot][c * _NA_CHUNK:(c + 1) * _NA_CHUNK, :] = p.astype(BF16)
        l_buf[slot][...] = jnp.broadcast_to(jnp.sum(l, axis=0, keepdims=True), l_buf[slot].shape)

    def values(blk, slot):
        qrows, krows, _ = geom(blk)
        ot = lax.dot_general(v_ref[krows, :], p_buf[slot][...], (((0,), (0,)), ((), ())),
                             preferred_element_type=F32)
        ot = ot * (1.0 / l_buf[slot][0:1, :])
        o_ref[qrows, :] = ot.T.astype(BF16)

    scores(0, 0)
    scores(1, 1)
    softmax(0)

    def body(j, carry):
        w = 2 * j
        scores(w + 2, 0)
        softmax(1)
        values(w, 0)
        scores(w + 3, 1)
        softmax(0)
        values(w + 1, 1)
        return carry

    lax.fori_loop(0, nblk // 2, body, 0)


def _neighborhood_attention(qkv, bias, layer, offsets):
    bsz, _, seq, hd = qkv.shape
    nh = N_HEADS_NA
    ncase, nk, nq = bias.shape[1:]

    def spec(off):
        return pl.BlockSpec((None, None, seq, hd), lambda b, h: (b, off + h, 0, 0))

    return pl.pallas_call(
        functools.partial(_na_kernel, offsets=offsets),
        grid=(bsz, nh),
        in_specs=[spec(0), spec(nh), spec(2 * nh),
                  pl.BlockSpec((None, ncase, nk, nq), lambda b, h: (layer * nh + h, 0, 0, 0))],
        out_specs=pl.BlockSpec((None, None, seq, hd), lambda b, h: (b, h, 0, 0)),
        out_shape=jax.ShapeDtypeStruct((bsz, nh, seq, hd), BF16),
        scratch_shapes=[pltpu.VMEM((nk, nq), F32), pltpu.VMEM((nk, nq), F32),
                        pltpu.VMEM((nk, nq), BF16), pltpu.VMEM((nk, nq), BF16),
                        pltpu.VMEM((16, nq), F32), pltpu.VMEM((16, nq), F32)],
        compiler_params=_cparams(("arbitrary", "arbitrary")),
        name="neighborhood_attention",
    )(qkv, qkv, qkv, bias)


def _out_kernel(a_ref, b_ref, x_ref, mod_ref, w_ref, o_ref, mix_s):
    na = a_ref.shape[0]
    for h in range(na):
        mix_s[:, h * HEAD_DIM:(h + 1) * HEAD_DIM] = a_ref[h]
    for h in range(b_ref.shape[0]):
        mix_s[:, (na + h) * HEAD_DIM:(na + h + 1) * HEAD_DIM] = b_ref[h]
    y = jnp.dot(mix_s[...], w_ref[...], preferred_element_type=F32)
    o_ref[...] = x_ref[...] + mod_ref[2:3, :] * y


def _out_proj(att_a, att_b, x, mod, w, layer, tm=512):
    bsz, seq, d = x.shape
    na, nb = att_a.shape[1], att_b.shape[1]
    return pl.pallas_call(
        _out_kernel,
        grid=(bsz, seq // tm),
        in_specs=[
            pl.BlockSpec((None, na, tm, HEAD_DIM), lambda b, i: (b, 0, i, 0)),
            pl.BlockSpec((None, nb, tm, HEAD_DIM), lambda b, i: (b, 0, i, 0)),
            pl.BlockSpec((None, tm, d), lambda b, i: (b, i, 0)),
            pl.BlockSpec((None, 6, d), lambda b, i: (b, 0, 0)),
            pl.BlockSpec((None, d, d), lambda b, i: (layer, 0, 0), pipeline_mode=pl.Buffered(1)),
        ],
        out_specs=pl.BlockSpec((None, tm, d), lambda b, i: (b, i, 0)),
        out_shape=jax.ShapeDtypeStruct((bsz, seq, d), F32),
        scratch_shapes=[pltpu.VMEM((tm, d), BF16)],
        compiler_params=_cparams(("arbitrary", "arbitrary")),
        name="attn_out_proj",
    )(att_a, att_b, x, mod, w)


def _mlp_in_kernel(x_ref, mod_ref, ln_ref, w_ref, o_ref, h_s):
    @pl.when(pl.program_id(2) == 0)
    def _():
        _norm_modulate(x_ref, ln_ref, mod_ref[4:5, :], mod_ref[3:4, :], h_s)

    y = jnp.maximum(jnp.dot(h_s[...], w_ref[...], preferred_element_type=F32), 0.0)
    o_ref[...] = (y * y).astype(BF16)


def _mlp_in(x, mod, ln, w, layer, tm=1024, tn=2048):
    bsz, seq, d = x.shape
    f = w.shape[2]
    return pl.pallas_call(
        _mlp_in_kernel,
        grid=(bsz, seq // tm, f // tn),
        in_specs=[
            pl.BlockSpec((None, tm, d), lambda b, i, j: (b, i, 0)),
            pl.BlockSpec((None, 6, d), lambda b, i, j: (b, 0, 0)),
            pl.BlockSpec((1, d), lambda b, i, j: (0, 0)),
            pl.BlockSpec((None, d, tn), lambda b, i, j: (layer, 0, j)),
        ],
        out_specs=pl.BlockSpec((None, tm, tn), lambda b, i, j: (b, i, j)),
        out_shape=jax.ShapeDtypeStruct((bsz, seq, f), BF16),
        scratch_shapes=[pltpu.VMEM((tm, d), BF16)],
        compiler_params=_cparams(("arbitrary", "arbitrary", "arbitrary")),
        name="mlp_in",
    )(x, mod, ln, w)


def _mlp_out_kernel(h_ref, x_ref, mod_ref, w_ref, o_ref):
    y = mod_ref[5:6, :] * jnp.dot(h_ref[...], w_ref[...], preferred_element_type=F32)

    @pl.when(pl.program_id(3) == 0)
    def _():
        o_ref[...] = x_ref[...] + y

    @pl.when(pl.program_id(3) != 0)
    def _():
        o_ref[...] += y


def _mlp_out(hid, x, mod, w, layer, tm=1024, tn=1024, tk=4096):
    bsz, seq, d = x.shape
    f = hid.shape[2]
    return pl.pallas_call(
        _mlp_out_kernel,
        grid=(bsz, seq // tm, d // tn, f // tk),
        in_specs=[
            pl.BlockSpec((None, tm, tk), lambda b, i, j, k: (b, i, k)),
            pl.BlockSpec((None, tm, tn), lambda b, i, j, k: (b, i, j)),
            pl.BlockSpec((None, 6, tn), lambda b, i, j, k: (b, 0, j)),
            pl.BlockSpec((None, tk, tn), lambda b, i, j, k: (layer, k, j)),
        ],
        out_specs=pl.BlockSpec((None, tm, tn), lambda b, i, j, k: (b, i, j)),
        out_shape=jax.ShapeDtypeStruct((bsz, seq, d), F32),
        compiler_params=pltpu.CompilerParams(dimension_semantics=("arbitrary",) * 4,
                                             vmem_limit_bytes=BIG_VMEM_LIMIT),
        name="mlp_out",
    )(hid, x, mod, w)


def _forward(x, c, ln1, w_ada, b_ada, w_in, q_norm_dil, k_norm_dil, q_norm_na, k_norm_na,
             na_rel_bias, w_out, ln2, w_mlp_in, w_mlp_out, dil_tq):
    bsz, seq, d = x.shape
    depth = w_ada.shape[0]
    rows = seq // GRID_W
    tables = _rope_tables(seq, _QKV_TM)
    mod_all = _modulation(c, w_ada, b_ada).reshape(depth, bsz, 6, d)
    w_in, w_out, w_mlp_in, w_mlp_out = (w.astype(BF16) for w in (w_in, w_out, w_mlp_in, w_mlp_out))
    na_bias = _na_bias_tables(na_rel_bias.reshape((-1,) + na_rel_bias.shape[2:]), rows)
    for layer in range(depth):
        mod = mod_all[layer]
        gains = jnp.stack([q_norm_dil[layer], k_norm_dil[layer],
                           q_norm_na[layer], k_norm_na[layer]]).astype(F32)
        qkv_dil, qkv_na = _qkv_proj(x, mod, ln1[layer][None, :], w_in, layer, gains, tables)
        att_dil = _dilated_attention(qkv_dil, tq=dil_tq)
        att_na = _neighborhood_attention(qkv_na, na_bias, layer, _na_offsets(rows))
        x = _out_proj(att_dil, att_na, x, mod, w_out, layer)
        hid = _mlp_in(x, mod, ln2[layer][None, :], w_mlp_in, layer)
        x = _mlp_out(hid, x, mod, w_mlp_out, layer)
    return x


def kernel(x, c, ln1, w_ada, b_ada, w_in, q_norm_dil, k_norm_dil, q_norm_na, k_norm_na,
           na_rel_bias, w_out, ln2, w_mlp_in, w_mlp_out):
    return _forward(x, c, ln1, w_ada, b_ada, w_in, q_norm_dil, k_norm_dil, q_norm_na,
                    k_norm_na, na_rel_bias, w_out, ln2, w_mlp_in, w_mlp_out, dil_tq=256)
```

```python
import functools

import numpy as np
import jax
import jax.numpy as jnp
from jax import lax
from jax.experimental import pallas as pl
from jax.experimental.pallas import tpu as pltpu

HEAD_DIM = 128
N_HEADS_DIL = 8
N_HEADS_NA = 8
ROT_DIM = HEAD_DIM // 4
ROPE_THETA = 500000.0
DIL_HALF = 64
DILATIONS = (16, 4, 1)
GRID_W = 64
NA_ROWS = 8
NA_COLS = 16
EPS = 1e-6
NEG_INF = -1e30
LOG2E = float(np.log2(np.e))

V7X_VMEM_BYTES = 64 * 1024 * 1024
VMEM_LIMIT = 56 * 1024 * 1024
BIG_VMEM_LIMIT = 60 * 1024 * 1024
DIL_VMEM_LIMIT = BIG_VMEM_LIMIT

F32 = jnp.float32
BF16 = jnp.bfloat16


def _cparams(sem):
    return pltpu.CompilerParams(dimension_semantics=sem, vmem_limit_bytes=VMEM_LIMIT)


def _mod_kernel(ct_ref, w_ref, b_ref, o_ref, act_s, *, nb):
    d = w_ref.shape[0]
    tn = w_ref.shape[1]

    @pl.when((pl.program_id(0) == 0) & (pl.program_id(1) == 0))
    def _():
        ct = ct_ref[...]
        act = ct * (1.0 / (1.0 + jnp.exp(-ct)))
        for b in range(nb):
            act_s[b] = jnp.broadcast_to(act[:, b:b + 1], (d, HEAD_DIM))

    def body(k, accs):
        r0 = pl.multiple_of(k * 8, 8)
        w8 = w_ref[pl.ds(r0, 8), :]
        out = []
        for b, acc in enumerate(accs):
            a = act_s[b, pl.ds(r0, 8), :]
            out.append(acc + w8 * jnp.concatenate([a] * (tn // HEAD_DIM), axis=1))
        return tuple(out)

    accs = lax.fori_loop(0, d // 8, body, tuple(jnp.zeros((8, tn), F32) for _ in range(nb)),
                         unroll=4)
    for b in range(nb):
        o_ref[b:b + 1, :] = jnp.sum(accs[b], axis=0, keepdims=True) + b_ref[...]


def _modulation(c, w_ada, b_ada):
    nl, d, n6 = w_ada.shape
    nb = c.shape[0]
    tn = 1024
    return pl.pallas_call(
        functools.partial(_mod_kernel, nb=nb),
        grid=(nl, n6 // tn),
        in_specs=[
            pl.BlockSpec((d, nb), lambda l, j: (0, 0)),
            pl.BlockSpec((None, d, tn), lambda l, j: (l, 0, j)),
            pl.BlockSpec((None, 1, tn), lambda l, j: (l, 0, j)),
        ],
        out_specs=pl.BlockSpec((None, nb, tn), lambda l, j: (l, 0, j)),
        out_shape=jax.ShapeDtypeStruct((nl, nb, n6), F32),
        scratch_shapes=[pltpu.VMEM((nb, d, HEAD_DIM), F32)],
        compiler_params=_cparams(("arbitrary", "arbitrary")),
        name="adaln_modulation",
    )(c.T, w_ada, b_ada.reshape(nl, 1, n6))


_NORM_CHUNK = 64


def _norm_modulate(x_ref, ln_ref, sc, sh, h_s, straight_line=False):
    tm = x_ref.shape[0]
    gain = ln_ref[...] * (1.0 + sc)

    def body(c, carry):
        r0 = c * _NORM_CHUNK if straight_line else pl.multiple_of(c * _NORM_CHUNK, _NORM_CHUNK)
        x = x_ref[pl.ds(r0, _NORM_CHUNK), :]
        ms = jnp.mean(x * x, axis=-1, keepdims=True)
        h_s[pl.ds(r0, _NORM_CHUNK), :] = (x * lax.rsqrt(ms + EPS) * gain + sh).astype(BF16)
        return carry

    if straight_line:
        for c in range(tm // _NORM_CHUNK):
            body(c, 0)
    else:
        lax.fori_loop(0, tm // _NORM_CHUNK, body, 0)


def _qkv_kernel(x_ref, xnext_ref, mod_ref, ln_ref, perm_ref, w_ref, gain_ref, cos_ref, sa_ref,
                sb_ref, oc_ref, on_ref, ha_s, hb_s, hc_s, acc_s):
    t = pl.program_id(1)

    @pl.when(t == 0)
    def _():
        _norm_modulate(x_ref, ln_ref, mod_ref[1:2, :], mod_ref[0:1, :], ha_s)

    @pl.when(t % 2 == 0)
    def _():
        _qkv_tile(xnext_ref, mod_ref, ln_ref, perm_ref, w_ref, gain_ref, cos_ref, sa_ref, sb_ref,
                  oc_ref, on_ref, ha_s, hb_s, hc_s, acc_s)

    @pl.when(t % 2 == 1)
    def _():
        _qkv_tile(xnext_ref, mod_ref, ln_ref, perm_ref, w_ref, gain_ref, cos_ref, sa_ref, sb_ref,
                  oc_ref, on_ref, hb_s, ha_s, hc_s, acc_s)


def _qkv_tile(xnext_ref, mod_ref, ln_ref, perm_ref, w_ref, gain_ref, cos_ref, sa_ref, sb_ref,
              oc_ref, on_ref, hn_s, hnext_s, hc_s, acc_s):
    scale = HEAD_DIM ** -0.5 * LOG2E
    nh = N_HEADS_DIL
    tn = nh * HEAD_DIM
    tm = hn_s.shape[0]
    ncls = oc_ref.shape[1]
    _norm_modulate(xnext_ref, ln_ref, mod_ref[1:2, :], mod_ref[0:1, :], hnext_s, straight_line=True)
    hc_s[...] = jnp.dot(perm_ref[...], hn_s[...], preferred_element_type=F32).astype(BF16)

    for g in range(w_ref.shape[1] // tn):
        rope = g < 3
        h_s = hc_s if rope else hn_s
        acc = acc_s.at[g % 2]
        acc[...] = jnp.dot(h_s[...], w_ref[:, g * tn:(g + 1) * tn], preferred_element_type=F32)
        kind = g % 3
        if kind < 2:
            row = 2 * (g // 3) + kind
            gain = gain_ref[row:row + 1, :] * (scale if kind == 0 else 1.0)
        for h in range(nh):
            y = acc[:, h * HEAD_DIM:(h + 1) * HEAD_DIM]
            if kind < 2:
                ms = jnp.mean(y * y, axis=-1, keepdims=True)
                y = y * lax.rsqrt(ms + EPS) * gain
            if rope and kind < 2:
                y = (y * cos_ref[...]
                     + pltpu.roll(y, ROT_DIM // 2, 1) * sa_ref[...]
                     + pltpu.roll(y, HEAD_DIM - ROT_DIM // 2, 1) * sb_ref[...])
            if rope:
                oc_ref[kind * nh + h] = y.astype(BF16).reshape(ncls, tm // ncls, HEAD_DIM)
            else:
                on_ref[kind * nh + h] = y.astype(BF16)


def _class_major_positions(seq, tm):
    dmax = DILATIONS[0]
    pos = np.arange(seq).reshape(seq // tm, tm // dmax, dmax)
    return pos.transpose(0, 2, 1).reshape(seq)


def _rope_tables(seq, tm):
    half = ROT_DIM // 2
    pos = _class_major_positions(seq, tm).astype(np.float64)[:, None]
    inv = ROPE_THETA ** (-np.arange(0, ROT_DIM, 2, dtype=np.float64) / ROT_DIM)[None, :]
    ang = pos * inv
    cos = np.ones((seq, HEAD_DIM), np.float64)
    sa = np.zeros((seq, HEAD_DIM), np.float64)
    sb = np.zeros((seq, HEAD_DIM), np.float64)
    cos[:, :half] = np.cos(ang)
    cos[:, half:ROT_DIM] = np.cos(ang)
    sa[:, half:ROT_DIM] = np.sin(ang)
    sb[:, :half] = -np.sin(ang)
    return (jnp.asarray(cos, F32), jnp.asarray(sa, F32), jnp.asarray(sb, F32))


_QKV_TM = 256


def _qkv_proj(x, mod, ln, w, layer, gains, tables, tm=_QKV_TM):
    bsz, seq, d = x.shape
    n = w.shape[2]
    tn = N_HEADS_DIL * HEAD_DIM
    nslab = n // HEAD_DIM // 2
    dmax = DILATIONS[0]
    cos, sa, sb = tables
    perm = _class_major_positions(tm, tm)
    perm_mat = jnp.asarray(np.eye(tm)[perm], BF16)
    tab_spec = pl.BlockSpec((tm, HEAD_DIM), lambda b, t: (t, 0))
    qkv_dil, qkv_na = pl.pallas_call(
        _qkv_kernel,
        grid=(bsz, seq // tm),
        in_specs=[
            pl.BlockSpec((None, tm, d), lambda b, t: (b, t, 0)),
            pl.BlockSpec((None, tm, d), lambda b, t: (b, jnp.minimum(t + 1, seq // tm - 1), 0)),
            pl.BlockSpec((None, 6, d), lambda b, t: (b, 0, 0)),
            pl.BlockSpec((1, d), lambda b, t: (0, 0)),
            pl.BlockSpec((tm, tm), lambda b, t: (0, 0)),
            pl.BlockSpec((None, d, n), lambda b, t: (layer, 0, 0), pipeline_mode=pl.Buffered(1)),
            pl.BlockSpec((4, HEAD_DIM), lambda b, t: (0, 0)),
            tab_spec, tab_spec, tab_spec,
        ],
        out_specs=[
            pl.BlockSpec((None, nslab, dmax, tm // dmax, HEAD_DIM), lambda b, t: (b, 0, 0, t, 0)),
            pl.BlockSpec((None, nslab, tm, HEAD_DIM), lambda b, t: (b, 0, t, 0)),
        ],
        out_shape=[jax.ShapeDtypeStruct((bsz, nslab, dmax, seq // dmax, HEAD_DIM), BF16),
                   jax.ShapeDtypeStruct((bsz, nslab, seq, HEAD_DIM), BF16)],
        scratch_shapes=[pltpu.VMEM((tm, d), BF16), pltpu.VMEM((tm, d), BF16),
                        pltpu.VMEM((tm, d), BF16), pltpu.VMEM((2, tm, tn), F32)],
        compiler_params=_cparams(("arbitrary", "arbitrary")),
        name="qkv_proj",
    )(x, x, mod, ln, perm_mat, w, gains, cos, sa, sb)
    return qkv_dil, qkv_na


_CAST_CHUNK = 256


def _dil_mask_tables(tq):
    kw = tq + 2 * DIL_HALF
    i = np.arange(tq)[:, None]
    j = np.arange(kw)[None, :]
    tabs = [np.where(np.abs(j - i - DIL_HALF * case) <= DIL_HALF, 0.0, NEG_INF) for case in range(3)]
    return jnp.asarray(np.stack(tabs), F32)


def _dil_kernel(q_ref, k_ref, v_ref, mask_ref, o_ref, a0, a1, a2, b0, b1, b2, c0, c1, c2,
                s0, s1, p0, p1, vb0, vb1, mb0, mb1, lb0, lb1, *, tq):
    seq = o_ref.shape[0]
    kw = tq + 2 * DIL_HALF
    s_buf, p_buf, v_buf, m_buf, l_buf = (s0, s1), (p0, p1), (vb0, vb1), (mb0, mb1), (lb0, lb1)
    set_a, set_b, set_c = (a0, a1, a2), (b0, b1, b2), (c0, c1, c2)
    step = DILATIONS[0] // DILATIONS[1]

    def finer_rows(dil, r, n0, size):
        fine = dil // step
        return pl.ds((r % fine) * (seq // fine) + r // fine + step * n0, size, stride=step)

    def class_rows(ref, cls_len, r, start, size, align):
        if len(ref.shape) == 3:
            return ref[r, pl.ds(pl.multiple_of(start, align), size), :]
        return ref[pl.ds(pl.multiple_of(r * cls_len + start, align), size), :]

    def reorder(dil, src, dst):
        cls_len = seq // dil
        nchunk = cls_len // _CAST_CHUNK

        def body(it, carry):
            r = it // nchunk
            n0 = (it % nchunk) * _CAST_CHUNK
            for s_ref, d_ref in zip(src, dst):
                chunk = class_rows(s_ref, cls_len, r, n0, _CAST_CHUNK, _CAST_CHUNK)
                d_ref[finer_rows(dil, r, n0, _CAST_CHUNK), :] = chunk.astype(F32)
            return carry

        lax.fori_loop(0, dil * nchunk, body, 0)

    def run_pattern(dil, mode, src, acc_in, acc_out):
        cls_len = seq // dil
        nblk = cls_len // tq
        total = dil * nblk
        qs, ks, vs = src

        def geom(it):
            it = jnp.minimum(it, total - 1)
            r = it // nblk
            n0 = (it % nblk) * tq
            nk0 = jnp.clip(n0 - DIL_HALF, 0, cls_len - kw)
            qrows = pl.ds(pl.multiple_of(r * cls_len + n0, tq), tq)
            return qrows, nk0, (n0 - nk0) // DIL_HALF, r, n0

        def scores(it, slot):
            _, nk0, case, r, n0 = geom(it)
            q = class_rows(qs, cls_len, r, n0, tq, tq).astype(BF16)
            k = class_rows(ks, cls_len, r, nk0, kw, DIL_HALF).astype(BF16)
            v_buf[slot][...] = class_rows(vs, cls_len, r, nk0, kw, DIL_HALF).astype(BF16)
            s = lax.dot_general(q, k, (((1,), (1,)), ((), ())), preferred_element_type=F32)
            s_buf[slot][...] = s + mask_ref[case]

        def softmax(slot):
            s = s_buf[slot][...]
            m = jnp.max(s, axis=-1, keepdims=True)
            p = jnp.exp2(s - m)
            l = jnp.sum(p, axis=-1, keepdims=True)
            p_buf[slot][...] = p.astype(BF16)
            m_buf[slot][...] = jnp.broadcast_to(m, (tq, HEAD_DIM))
            l_buf[slot][...] = jnp.broadcast_to(l, (tq, HEAD_DIM))

        def values(it, slot):
            qrows, _, _, r, n0 = geom(it)
            o = jnp.dot(p_buf[slot][...], v_buf[slot][...], preferred_element_type=F32)
            mb = m_buf[slot][...]
            lb = l_buf[slot][...]
            if mode != "init":
                acc_i, m_i, l_i = acc_in
                m_old = m_i[qrows, :]
                m_new = jnp.maximum(m_old, mb)
                a_old = jnp.exp2(m_old - m_new)
                a_new = jnp.exp2(mb - m_new)
                o = acc_i[qrows, :] * a_old + o * a_new
                lb = l_i[qrows, :] * a_old + lb * a_new
                mb = m_new
            if mode == "final":
                o_ref[qrows, :] = (o / lb).astype(BF16)
            else:
                acc_o, m_o, l_o = acc_out
                orows = finer_rows(dil, r, n0, tq)
                acc_o[orows, :] = o
                m_o[orows, :] = mb
                l_o[orows, :] = lb

        scores(0, 0)
        scores(1, 1)
        softmax(0)

        def body(j, carry):
            w = 2 * j
            values(w, 0)
            scores(w + 2, 0)
            softmax(1)
            values(w + 1, 1)
            scores(w + 3, 1)
            softmax(0)
            return carry

        lax.fori_loop(0, total // 2, body, 0)

    d16, d4, d1 = DILATIONS
    run_pattern(d16, "init", (q_ref, k_ref, v_ref), None, set_a)
    reorder(d16, (q_ref, k_ref, v_ref), set_b)
    run_pattern(d4, "merge", set_b, set_a, set_c)
    reorder(d4, set_b, set_a)
    run_pattern(d1, "final", set_a, set_c, None)


def _dilated_attention(qkv, tq=256):
    bsz, _, ncls, cls_len, hd = qkv.shape
    seq = ncls * cls_len
    nh = N_HEADS_DIL
    kw = tq + 2 * DIL_HALF
    masks = _dil_mask_tables(tq)

    def spec(off):
        return pl.BlockSpec((None, None, ncls, cls_len, hd), lambda b, h: (b, off + h, 0, 0, 0))

    def two(shape, dtype):
        return [pltpu.VMEM(shape, dtype), pltpu.VMEM(shape, dtype)]

    return pl.pallas_call(
        functools.partial(_dil_kernel, tq=tq),
        grid=(bsz, nh),
        in_specs=[spec(0), spec(nh), spec(2 * nh),
                  pl.BlockSpec(masks.shape, lambda b, h: (0, 0, 0), pipeline_mode=pl.Buffered(1))],
        out_specs=pl.BlockSpec((None, None, seq, hd), lambda b, h: (b, h, 0, 0)),
        out_shape=jax.ShapeDtypeStruct((bsz, nh, seq, hd), BF16),
        scratch_shapes=([pltpu.VMEM((seq, hd), F32) for _ in range(9)]
                        + two((tq, kw), F32) + two((tq, kw), BF16) + two((kw, hd), BF16)
                        + two((tq, hd), F32) + two((tq, hd), F32)),
        compiler_params=pltpu.CompilerParams(dimension_semantics=("arbitrary", "arbitrary"),
                                             vmem_limit_bytes=DIL_VMEM_LIMIT),
        name="dilated_attention",
    )(qkv, qkv, qkv, masks)


_NA_QROWS = 4
_NA_KROWS = NA_ROWS + _NA_QROWS - 1
_NA_CHUNK = 64


def _na_key_start(i0, rows):
    return np.clip(i0 - NA_ROWS // 2, 0, rows - _NA_KROWS)


def _na_offsets(rows):
    return tuple(sorted({int(i0 - _na_key_start(i0, rows)) for i0 in range(0, rows, _NA_QROWS)}))


def _na_bias_tables(rpb, rows):
    nq = _NA_QROWS * GRID_W
    nk = _NA_KROWS * GRID_W
    nh, nrow, ncol = rpb.shape
    w = GRID_W
    u = jnp.pad(rpb.astype(F32) * LOG2E, ((0, 0), (0, 0), (w - NA_COLS, w + NA_COLS - ncol)))
    toep = jnp.broadcast_to(u[:, :, None, :], (nh, nrow, w, 2 * w)).reshape(nh, nrow, 2 * w * w)
    toep = toep[:, :, :w * (2 * w - 1)].reshape(nh, nrow, w, 2 * w - 1)[:, :, :, w - 1:]
    cs = np.clip(np.arange(w) - NA_COLS // 2, 0, w - NA_COLS)
    cvalid = (np.arange(w)[None, :] >= cs[:, None]) & (np.arange(w)[None, :] < cs[:, None] + NA_COLS)
    toep = jnp.pad(toep, ((0, 0), (_NA_KROWS, _NA_KROWS), (0, 0), (0, 0)))

    def row_pattern(i0):
        ks = int(_na_key_start(i0, rows))
        first, rvalid = [], []
        for qi in range(i0, i0 + _NA_QROWS):
            rs = int(np.clip(qi - NA_ROWS // 2, 0, rows - NA_ROWS))
            first.append(ks - qi + NA_ROWS - 1 + _NA_KROWS)
            rvalid += [rs <= kr < rs + NA_ROWS for kr in range(ks, ks + _NA_KROWS)]
        return first, rvalid

    slabs, rvalids = [], []
    for off in _na_offsets(rows):
        blocks = [i0 for i0 in range(0, rows, _NA_QROWS) if i0 - _na_key_start(i0, rows) == off]
        first, rvalid = row_pattern(blocks[0])
        assert all(row_pattern(i0) == (first, rvalid) for i0 in blocks)
        slabs += [toep[:, lo:lo + _NA_KROWS] for lo in first]
        rvalids += rvalid
    ncase = len(_na_offsets(rows))
    tab = jnp.stack(slabs, axis=1).reshape(nh, ncase, _NA_QROWS, _NA_KROWS, w, w)
    rvalid = np.asarray(rvalids).reshape(ncase, _NA_QROWS, _NA_KROWS)
    valid = rvalid[:, :, :, None, None] & cvalid[None, None, None]
    tab = jnp.where(valid[None], tab, NEG_INF)
    return tab.transpose(0, 1, 3, 5, 2, 4).reshape(nh, ncase, nk, nq)


def _na_kernel(q_ref, k_ref, v_ref, bias_ref, o_ref, s0, s1, p0, p1, l0, l1, *, offsets):
    seq = q_ref.shape[0]
    rows = seq // GRID_W
    nq = _NA_QROWS * GRID_W
    nk = _NA_KROWS * GRID_W
    nblk = rows // _NA_QROWS
    s_buf, p_buf, l_buf = (s0, s1), (p0, p1), (l0, l1)

    def geom(blk):
        blk = jnp.minimum(blk, nblk - 1)
        i0 = blk * _NA_QROWS
        ks = jnp.clip(i0 - NA_ROWS // 2, 0, rows - _NA_KROWS)
        case = sum((i0 - ks >= off).astype(jnp.int32) for off in offsets[1:])
        qrows = pl.ds(pl.multiple_of(i0 * GRID_W, nq), nq)
        krows = pl.ds(pl.multiple_of(ks * GRID_W, GRID_W), nk)
        return qrows, krows, case

    def scores(blk, slot):
        qrows, krows, case = geom(blk)
        st = lax.dot_general(k_ref[krows, :], q_ref[qrows, :], (((1,), (1,)), ((), ())),
                             preferred_element_type=F32)
        s_buf[slot][...] = st + bias_ref[case]

    def softmax(slot):
        s = s_buf[slot]
        nchunk = nk // _NA_CHUNK
        m = s[0:_NA_CHUNK, :]
        for c in range(1, nchunk):
            m = jnp.maximum(m, s[c * _NA_CHUNK:(c + 1) * _NA_CHUNK, :])
        m = jnp.max(m, axis=0, keepdims=True)
        l = jnp.zeros((_NA_CHUNK, nq), F32)
        for c in range(nchunk):
            p = jnp.exp2(s[c * _NA_CHUNK:(c + 1) * _NA_CHUNK, :] - m)
            l = l + p
            p_buf[slot][c * _NA_CHUNK:(c + 1) * _NA_CHUNK, :] = p.astype(BF16)
        l_buf[slot][...] = jnp.broadcast_to(jnp.sum(l, axis=0, keepdims=True), l_buf[slot].shape)

    def values(blk, slot):
        qrows, krows, _ = geom(blk)
        ot = lax.dot_general(v_ref[krows, :], p_buf[slot][...], (((0,), (0,)), ((), ())),
                             preferred_element_type=F32)
        ot = ot * (1.0 / l_buf[slot][0:1, :])
        o_ref[qrows, :] = ot.T.astype(BF16)

    scores(0, 0)
    scores(1, 1)
    softmax(0)

    def body(j, carry):
        w = 2 * j
        scores(w + 2, 0)
        softmax(1)
        values(w, 0)
        scores(w + 3, 1)
        softmax(0)
        values(w + 1, 1)
        return carry

    lax.fori_loop(0, nblk // 2, body, 0)


def _neighborhood_attention(qkv, bias, layer, offsets):
    bsz, _, seq, hd = qkv.shape
    nh = N_HEADS_NA
    ncase, nk, nq = bias.shape[1:]

    def spec(off):
        return pl.BlockSpec((None, None, seq, hd), lambda b, h: (b, off + h, 0, 0))

    return pl.pallas_call(
        functools.partial(_na_kernel, offsets=offsets),
        grid=(bsz, nh),
        in_specs=[spec(0), spec(nh), spec(2 * nh),
                  pl.BlockSpec((None, ncase, nk, nq), lambda b, h: (layer * nh + h, 0, 0, 0))],
        out_specs=pl.BlockSpec((None, None, seq, hd), lambda b, h: (b, h, 0, 0)),
        out_shape=jax.ShapeDtypeStruct((bsz, nh, seq, hd), BF16),
        scratch_shapes=[pltpu.VMEM((nk, nq), F32), pltpu.VMEM((nk, nq), F32),
                        pltpu.VMEM((nk, nq), BF16), pltpu.VMEM((nk, nq), BF16),
                        pltpu.VMEM((16, nq), F32), pltpu.VMEM((16, nq), F32)],
        compiler_params=_cparams(("arbitrary", "arbitrary")),
        name="neighborhood_attention",
    )(qkv, qkv, qkv, bias)


def _out_kernel(a_ref, b_ref, x_ref, mod_ref, w_ref, o_ref, mix_s):
    na = a_ref.shape[0]
    for h in range(na):
        mix_s[:, h * HEAD_DIM:(h + 1) * HEAD_DIM] = a_ref[h]
    for h in range(b_ref.shape[0]):
        mix_s[:, (na + h) * HEAD_DIM:(na + h + 1) * HEAD_DIM] = b_ref[h]
    y = jnp.dot(mix_s[...], w_ref[...], preferred_element_type=F32)
    o_ref[...] = x_ref[...] + mod_ref[2:3, :] * y


def _out_proj(att_a, att_b, x, mod, w, layer, tm=512):
    bsz, seq, d = x.shape
    na, nb = att_a.shape[1], att_b.shape[1]
    return pl.pallas_call(
        _out_kernel,
        grid=(bsz, seq // tm),
        in_specs=[
            pl.BlockSpec((None, na, tm, HEAD_DIM), lambda b, i: (b, 0, i, 0)),
            pl.BlockSpec((None, nb, tm, HEAD_DIM), lambda b, i: (b, 0, i, 0)),
            pl.BlockSpec((None, tm, d), lambda b, i: (b, i, 0)),
            pl.BlockSpec((None, 6, d), lambda b, i: (b, 0, 0)),
            pl.BlockSpec((None, d, d), lambda b, i: (layer, 0, 0), pipeline_mode=pl.Buffered(1)),
        ],
        out_specs=pl.BlockSpec((None, tm, d), lambda b, i: (b, i, 0)),
        out_shape=jax.ShapeDtypeStruct((bsz, seq, d), F32),
        scratch_shapes=[pltpu.VMEM((tm, d), BF16)],
        compiler_params=_cparams(("arbitrary", "arbitrary")),
        name="attn_out_proj",
    )(att_a, att_b, x, mod, w)


def _mlp_in_kernel(x_ref, mod_ref, ln_ref, w_ref, o_ref, h_s):
    @pl.when(pl.program_id(2) == 0)
    def _():
        _norm_modulate(x_ref, ln_ref, mod_ref[4:5, :], mod_ref[3:4, :], h_s)

    y = jnp.maximum(jnp.dot(h_s[...], w_ref[...], preferred_element_type=F32), 0.0)
    o_ref[...] = (y * y).astype(BF16)


def _mlp_in(x, mod, ln, w, layer, tm=1024, tn=2048):
    bsz, seq, d = x.shape
    f = w.shape[2]
    return pl.pallas_call(
        _mlp_in_kernel,
        grid=(bsz, seq // tm, f // tn),
        in_specs=[
            pl.BlockSpec((None, tm, d), lambda b, i, j: (b, i, 0)),
            pl.BlockSpec((None, 6, d), lambda b, i, j: (b, 0, 0)),
            pl.BlockSpec((1, d), lambda b, i, j: (0, 0)),
            pl.BlockSpec((None, d, tn), lambda b, i, j: (layer, 0, j)),
        ],
        out_specs=pl.BlockSpec((None, tm, tn), lambda b, i, j: (b, i, j)),
        out_shape=jax.ShapeDtypeStruct((bsz, seq, f), BF16),
        scratch_shapes=[pltpu.VMEM((tm, d), BF16)],
        compiler_params=_cparams(("arbitrary", "arbitrary", "arbitrary")),
        name="mlp_in",
    )(x, mod, ln, w)


def _mlp_out_kernel(h_ref, x_ref, mod_ref, w_ref, o_ref):
    y = mod_ref[5:6, :] * jnp.dot(h_ref[...], w_ref[...], preferred_element_type=F32)

    @pl.when(pl.program_id(3) == 0)
    def _():
        o_ref[...] = x_ref[...] + y

    @pl.when(pl.program_id(3) != 0)
    def _():
        o_ref[...] += y


def _mlp_out(hid, x, mod, w, layer, tm=1024, tn=1024, tk=4096):
    bsz, seq, d = x.shape
    f = hid.shape[2]
    return pl.pallas_call(
        _mlp_out_kernel,
        grid=(bsz, seq // tm, d // tn, f // tk),
        in_specs=[
            pl.BlockSpec((None, tm, tk), lambda b, i, j, k: (b, i, k)),
            pl.BlockSpec((None, tm, tn), lambda b, i, j, k: (b, i, j)),
            pl.BlockSpec((None, 6, tn), lambda b, i, j, k: (b, 0, j)),
            pl.BlockSpec((None, tk, tn), lambda b, i, j, k: (layer, k, j)),
        ],
        out_specs=pl.BlockSpec((None, tm, tn), lambda b, i, j, k: (b, i, j)),
        out_shape=jax.ShapeDtypeStruct((bsz, seq, d), F32),
        compiler_params=pltpu.CompilerParams(dimension_semantics=("arbitrary",) * 4,
                                             vmem_limit_bytes=BIG_VMEM_LIMIT),
        name="mlp_out",
    )(hid, x, mod, w)


def _forward(x, c, ln1, w_ada, b_ada, w_in, q_norm_dil, k_norm_dil, q_norm_na, k_norm_na,
             na_rel_bias, w_out, ln2, w_mlp_in, w_mlp_out, dil_tq):
    bsz, seq, d = x.shape
    depth = w_ada.shape[0]
    rows = seq // GRID_W
    tables = _rope_tables(seq, _QKV_TM)
    mod_all = _modulation(c, w_ada, b_ada).reshape(depth, bsz, 6, d)
    w_in, w_out, w_mlp_in, w_mlp_out = (w.astype(BF16) for w in (w_in, w_out, w_mlp_in, w_mlp_out))
    na_bias = _na_bias_tables(na_rel_bias.reshape((-1,) + na_rel_bias.shape[2:]), rows)
    for layer in range(depth):
        mod = mod_all[layer]
        gains = jnp.stack([q_norm_dil[layer], k_norm_dil[layer],
                           q_norm_na[layer], k_norm_na[layer]]).astype(F32)
        qkv_dil, qkv_na = _qkv_proj(x, mod, ln1[layer][None, :], w_in, layer, gains, tables)
        att_dil = _dilated_attention(qkv_dil, tq=dil_tq)
        att_na = _neighborhood_attention(qkv_na, na_bias, layer, _na_offsets(rows))
        x = _out_proj(att_dil, att_na, x, mod, w_out, layer)
        hid = _mlp_in(x, mod, ln2[layer][None, :], w_mlp_in, layer)
        x = _mlp_out(hid, x, mod, w_mlp_out, layer)
    return x


def kernel(x, c, ln1, w_ada, b_ada, w_in, q_norm_dil, k_norm_dil, q_norm_na, k_norm_na,
           na_rel_bias, w_out, ln2, w_mlp_in, w_mlp_out):
    return _forward(x, c, ln1, w_ada, b_ada, w_in, q_norm_dil, k_norm_dil, q_norm_na,
                    k_norm_na, na_rel_bias, w_out, ln2, w_mlp_in, w_mlp_out, dil_tq=256)
```

```python
import functools

import numpy as np
import jax
import jax.numpy as jnp
from jax import lax
from jax.experimental import pallas as pl
from jax.experimental.pallas import tpu as pltpu

HEAD_DIM = 128
N_HEADS_DIL = 8
N_HEADS_NA = 8
ROT_DIM = HEAD_DIM // 4
ROPE_THETA = 500000.0
DIL_HALF = 64
DILATIONS = (16, 4, 1)
GRID_W = 64
NA_ROWS = 8
NA_COLS = 16
EPS = 1e-6
NEG_INF = -1e30
LOG2E = float(np.log2(np.e))

V7X_VMEM_BYTES = 64 * 1024 * 1024
VMEM_LIMIT = 56 * 1024 * 1024
BIG_VMEM_LIMIT = 60 * 1024 * 1024
DIL_VMEM_LIMIT = BIG_VMEM_LIMIT

F32 = jnp.float32
BF16 = jnp.bfloat16


def _cparams(sem):
    return pltpu.CompilerParams(dimension_semantics=sem, vmem_limit_bytes=VMEM_LIMIT)


def _mod_kernel(ct_ref, w_ref, b_ref, o_ref, act_s, *, nb):
    d = w_ref.shape[0]
    tn = w_ref.shape[1]

    @pl.when((pl.program_id(0) == 0) & (pl.program_id(1) == 0))
    def _():
        ct = ct_ref[...]
        act = ct * (1.0 / (1.0 + jnp.exp(-ct)))
        for b in range(nb):
            act_s[b] = jnp.broadcast_to(act[:, b:b + 1], (d, HEAD_DIM))

    def body(k, accs):
        r0 = pl.multiple_of(k * 8, 8)
        w8 = w_ref[pl.ds(r0, 8), :]
        out = []
        for b, acc in enumerate(accs):
            a = act_s[b, pl.ds(r0, 8), :]
            out.append(acc + w8 * jnp.concatenate([a] * (tn // HEAD_DIM), axis=1))
        return tuple(out)

    accs = lax.fori_loop(0, d // 8, body, tuple(jnp.zeros((8, tn), F32) for _ in range(nb)),
                         unroll=4)
    for b in range(nb):
        o_ref[b:b + 1, :] = jnp.sum(accs[b], axis=0, keepdims=True) + b_ref[...]


def _modulation(c, w_ada, b_ada):
    nl, d, n6 = w_ada.shape
    nb = c.shape[0]
    tn = 1024
    return pl.pallas_call(
        functools.partial(_mod_kernel, nb=nb),
        grid=(nl, n6 // tn),
        in_specs=[
            pl.BlockSpec((d, nb), lambda l, j: (0, 0)),
            pl.BlockSpec((None, d, tn), lambda l, j: (l, 0, j)),
            pl.BlockSpec((None, 1, tn), lambda l, j: (l, 0, j)),
        ],
        out_specs=pl.BlockSpec((None, nb, tn), lambda l, j: (l, 0, j)),
        out_shape=jax.ShapeDtypeStruct((nl, nb, n6), F32),
        scratch_shapes=[pltpu.VMEM((nb, d, HEAD_DIM), F32)],
        compiler_params=_cparams(("arbitrary", "arbitrary")),
        name="adaln_modulation",
    )(c.T, w_ada, b_ada.reshape(nl, 1, n6))


_NORM_CHUNK = 64


def _norm_modulate(x_ref, ln_ref, sc, sh, h_s, straight_line=False):
    tm = x_ref.shape[0]
    gain = ln_ref[...] * (1.0 + sc)

    def body(c, carry):
        r0 = c * _NORM_CHUNK if straight_line else pl.multiple_of(c * _NORM_CHUNK, _NORM_CHUNK)
        x = x_ref[pl.ds(r0, _NORM_CHUNK), :]
        ms = jnp.mean(x * x, axis=-1, keepdims=True)
        h_s[pl.ds(r0, _NORM_CHUNK), :] = (x * lax.rsqrt(ms + EPS) * gain + sh).astype(BF16)
        return carry

    if straight_line:
        for c in range(tm // _NORM_CHUNK):
            body(c, 0)
    else:
        lax.fori_loop(0, tm // _NORM_CHUNK, body, 0)


def _qkv_kernel(x_ref, xnext_ref, mod_ref, ln_ref, perm_ref, w_ref, gain_ref, cos_ref, sa_ref,
                sb_ref, oc_ref, on_ref, ha_s, hb_s, hc_s, acc_s):
    t = pl.program_id(1)

    @pl.when(t == 0)
    def _():
        _norm_modulate(x_ref, ln_ref, mod_ref[1:2, :], mod_ref[0:1, :], ha_s)

    @pl.when(t % 2 == 0)
    def _():
        _qkv_tile(xnext_ref, mod_ref, ln_ref, perm_ref, w_ref, gain_ref, cos_ref, sa_ref, sb_ref,
                  oc_ref, on_ref, ha_s, hb_s, hc_s, acc_s)

    @pl.when(t % 2 == 1)
    def _():
        _qkv_tile(xnext_ref, mod_ref, ln_ref, perm_ref, w_ref, gain_ref, cos_ref, sa_ref, sb_ref,
                  oc_ref, on_ref, hb_s, ha_s, hc_s, acc_s)


def _qkv_tile(xnext_ref, mod_ref, ln_ref, perm_ref, w_ref, gain_ref, cos_ref, sa_ref, sb_ref,
              oc_ref, on_ref, hn_s, hnext_s, hc_s, acc_s):
    scale = HEAD_DIM ** -0.5 * LOG2E
    nh = N_HEADS_DIL
    tn = nh * HEAD_DIM
    tm = hn_s.shape[0]
    ncls = oc_ref.shape[1]
    _norm_modulate(xnext_ref, ln_ref, mod_ref[1:2, :], mod_ref[0:1, :], hnext_s, straight_line=True)
    hc_s[...] = jnp.dot(perm_ref[...], hn_s[...], preferred_element_type=F32).astype(BF16)

    for g in range(w_ref.shape[1] // tn):
        rope = g < 3
        h_s = hc_s if rope else hn_s
        acc = acc_s.at[g % 2]
        acc[...] = jnp.dot(h_s[...], w_ref[:, g * tn:(g + 1) * tn], preferred_element_type=F32)
        kind = g % 3
        if kind < 2:
            row = 2 * (g // 3) + kind
            gain = gain_ref[row:row + 1, :] * (scale if kind == 0 else 1.0)
        for h in range(nh):
            y = acc[:, h * HEAD_DIM:(h + 1) * HEAD_DIM]
            if kind < 2:
                ms = jnp.mean(y * y, axis=-1, keepdims=True)
                y = y * lax.rsqrt(ms + EPS) * gain
            if rope and kind < 2:
                y = (y * cos_ref[...]
                     + pltpu.roll(y, ROT_DIM // 2, 1) * sa_ref[...]
                     + pltpu.roll(y, HEAD_DIM - ROT_DIM // 2, 1) * sb_ref[...])
            if rope:
                oc_ref[kind * nh + h] = y.astype(BF16).reshape(ncls, tm // ncls, HEAD_DIM)
            else:
                on_ref[kind * nh + h] = y.astype(BF16)


def _class_major_positions(seq, tm):
    dmax = DILATIONS[0]
    pos = np.arange(seq).reshape(seq // tm, tm // dmax, dmax)
    return pos.transpose(0, 2, 1).reshape(seq)


def _rope_tables(seq, tm):
    half = ROT_DIM // 2
    pos = _class_major_positions(seq, tm).astype(np.float64)[:, None]
    inv = ROPE_THETA ** (-np.arange(0, ROT_DIM, 2, dtype=np.float64) / ROT_DIM)[None, :]
    ang = pos * inv
    cos = np.ones((seq, HEAD_DIM), np.float64)
    sa = np.zeros((seq, HEAD_DIM), np.float64)
    sb = np.zeros((seq, HEAD_DIM), np.float64)
    cos[:, :half] = np.cos(ang)
    cos[:, half:ROT_DIM] = np.cos(ang)
    sa[:, half:ROT_DIM] = np.sin(ang)
    sb[:, :half] = -np.sin(ang)
    return (jnp.asarray(cos, F32), jnp.asarray(sa, F32), jnp.asarray(sb, F32))


_QKV_TM = 256


def _qkv_proj(x, mod, ln, w, layer, gains, tables, tm=_QKV_TM):
    bsz, seq, d = x.shape
    n = w.shape[2]
    tn = N_HEADS_DIL * HEAD_DIM
    nslab = n // HEAD_DIM // 2
    dmax = DILATIONS[0]
    cos, sa, sb = tables
    perm = _class_major_positions(tm, tm)
    perm_mat = jnp.asarray(np.eye(tm)[perm], BF16)
    tab_spec = pl.BlockSpec((tm, HEAD_DIM), lambda b, t: (t, 0))
    qkv_dil, qkv_na = pl.pallas_call(
        _qkv_kernel,
        grid=(bsz, seq // tm),
        in_specs=[
            pl.BlockSpec((None, tm, d), lambda b, t: (b, t, 0)),
            pl.BlockSpec((None, tm, d), lambda b, t: (b, jnp.minimum(t + 1, seq // tm - 1), 0)),
            pl.BlockSpec((None, 6, d), lambda b, t: (b, 0, 0)),
            pl.BlockSpec((1, d), lambda b, t: (0, 0)),
            pl.BlockSpec((tm, tm), lambda b, t: (0, 0)),
            pl.BlockSpec((None, d, n), lambda b, t: (layer, 0, 0), pipeline_mode=pl.Buffered(1)),
            pl.BlockSpec((4, HEAD_DIM), lambda b, t: (0, 0)),
            tab_spec, tab_spec, tab_spec,
        ],
        out_specs=[
            pl.BlockSpec((None, nslab, dmax, tm // dmax, HEAD_DIM), lambda b, t: (b, 0, 0, t, 0)),
            pl.BlockSpec((None, nslab, tm, HEAD_DIM), lambda b, t: (b, 0, t, 0)),
        ],
        out_shape=[jax.ShapeDtypeStruct((bsz, nslab, dmax, seq // dmax, HEAD_DIM), BF16),
                   jax.ShapeDtypeStruct((bsz, nslab, seq, HEAD_DIM), BF16)],
        scratch_shapes=[pltpu.VMEM((tm, d), BF16), pltpu.VMEM((tm, d), BF16),
                        pltpu.VMEM((tm, d), BF16), pltpu.VMEM((2, tm, tn), F32)],
        compiler_params=_cparams(("arbitrary", "arbitrary")),
        name="qkv_proj",
    )(x, x, mod, ln, perm_mat, w, gains, cos, sa, sb)
    return qkv_dil, qkv_na


_CAST_CHUNK = 256
_DIL_STREAMS = 4


def _dil_mask_tables(tq):
    kw = tq + 2 * DIL_HALF
    i = np.arange(tq)[:, None]
    j = np.arange(kw)[None, :]
    tabs = [np.where(np.abs(j - i - DIL_HALF * case) <= DIL_HALF, 0.0, NEG_INF) for case in range(3)]
    return jnp.asarray(np.stack(tabs), F32)


def _dil_kernel(q_ref, k_ref, v_ref, mask_ref, o_ref, a0, a1, a2, b0, b1, b2, c0, c1, c2,
                *stage_bufs, tq, nstream):
    seq = o_ref.shape[0]
    kw = tq + 2 * DIL_HALF
    nbuf = 2 * nstream
    s_buf, p_buf, v_buf, m_buf, l_buf = (stage_bufs[i * nbuf:(i + 1) * nbuf] for i in range(5))
    set_a, set_b, set_c = (a0, a1, a2), (b0, b1, b2), (c0, c1, c2)
    step = DILATIONS[0] // DILATIONS[1]

    def finer_rows(dil, r, n0, size):
        fine = dil // step
        return pl.ds((r % fine) * (seq // fine) + r // fine + step * n0, size, stride=step)

    def class_rows(ref, cls_len, r, start, size, align):
        if len(ref.shape) == 3:
            return ref[r, pl.ds(pl.multiple_of(start, align), size), :]
        return ref[pl.ds(pl.multiple_of(r * cls_len + start, align), size), :]

    def reorder_chunk(dil, src, dst, it):
        cls_len = seq // dil
        nchunk = cls_len // _CAST_CHUNK
        r = it // nchunk
        n0 = (it % nchunk) * _CAST_CHUNK
        for s_ref, d_ref in zip(src, dst):
            chunk = class_rows(s_ref, cls_len, r, n0, _CAST_CHUNK, _CAST_CHUNK)
            d_ref[finer_rows(dil, r, n0, _CAST_CHUNK), :] = chunk.astype(F32)

    def reorder(dil, src, dst):
        def body(it, carry):
            reorder_chunk(dil, src, dst, it)
            return carry

        lax.fori_loop(0, seq // _CAST_CHUNK, body, 0)

    def run_pattern(dil, mode, src, acc_in, acc_out, side_work=None, side_chunks=0):
        cls_len = seq // dil
        nblk = cls_len // tq
        total = dil * nblk
        per_stream = total // nstream
        qs, ks, vs = src

        def geom(g, it):
            it = g * per_stream + jnp.minimum(it, per_stream - 1)
            r = it // nblk
            n0 = (it % nblk) * tq
            nk0 = jnp.clip(n0 - DIL_HALF, 0, cls_len - kw)
            qrows = pl.ds(pl.multiple_of(r * cls_len + n0, tq), tq)
            return qrows, nk0, (n0 - nk0) // DIL_HALF, r, n0

        def scores(g, it, slot):
            slot = 2 * g + slot
            _, nk0, case, r, n0 = geom(g, it)
            q = class_rows(qs, cls_len, r, n0, tq, tq).astype(BF16)
            k = class_rows(ks, cls_len, r, nk0, kw, DIL_HALF).astype(BF16)
            v_buf[slot][...] = class_rows(vs, cls_len, r, nk0, kw, DIL_HALF).astype(BF16)
            s = lax.dot_general(q, k, (((1,), (1,)), ((), ())), preferred_element_type=F32)
            s_buf[slot][...] = s + mask_ref[case]

        def softmax(g, slot):
            slot = 2 * g + slot
            s = s_buf[slot][...]
            m = jnp.max(s, axis=-1, keepdims=True)
            p = jnp.exp2(s - m)
            l = jnp.sum(p, axis=-1, keepdims=True)
            p_buf[slot][...] = p.astype(BF16)
            m_buf[slot][...] = jnp.broadcast_to(m, (tq, HEAD_DIM))
            l_buf[slot][...] = jnp.broadcast_to(l, (tq, HEAD_DIM))

        def values(g, it, slot):
            slot = 2 * g + slot
            qrows, _, _, r, n0 = geom(g, it)
            o = jnp.dot(p_buf[slot][...], v_buf[slot][...], preferred_element_type=F32)
            mb = m_buf[slot][...]
            lb = l_buf[slot][...]
            if mode != "init":
                acc_i, m_i, l_i = acc_in
                m_old = m_i[qrows, :]
                m_new = jnp.maximum(m_old, mb)
                a_old = jnp.exp2(m_old - m_new)
                a_new = jnp.exp2(mb - m_new)
                o = acc_i[qrows, :] * a_old + o * a_new
                lb = l_i[qrows, :] * a_old + lb * a_new
                mb = m_new
            if mode == "final":
                o_ref[qrows, :] = (o / lb).astype(BF16)
            else:
                acc_o, m_o, l_o = acc_out
                orows = finer_rows(dil, r, n0, tq)
                acc_o[orows, :] = o
                m_o[orows, :] = mb
                l_o[orows, :] = lb

        for g in range(nstream):
            scores(g, 0, 0)
            scores(g, 1, 1)
            softmax(g, 0)

        def body(j, carry):
            w = 2 * j
            for g in range(nstream):
                values(g, w, 0)
                scores(g, w + 2, 0)
                softmax(g, 1)
                values(g, w + 1, 1)
                scores(g, w + 3, 1)
                softmax(g, 0)
                if side_work is not None:
                    for c in range(side_chunks):
                        side_work((g * per_stream + w) * side_chunks // 2 + c)
            return carry

        lax.fori_loop(0, per_stream // 2, body, 0)

    d16, d4, d1 = DILATIONS
    inputs = (q_ref, k_ref, v_ref)
    nchunks, nblocks = seq // _CAST_CHUNK, seq // tq
    if (2 * nchunks) % nblocks == 0 and (nblocks // 2) % nstream == 0:
        run_pattern(d16, "init", inputs, None, set_a, side_chunks=2 * nchunks // nblocks,
                    side_work=lambda i: reorder_chunk(d16, inputs, set_b, i))
    else:
        run_pattern(d16, "init", inputs, None, set_a)
        reorder(d16, inputs, set_b)
    run_pattern(d4, "merge", set_b, set_a, set_c)
    reorder(d4, set_b, set_a)
    run_pattern(d1, "final", set_a, set_c, None)


def _dilated_attention(qkv, tq=256, nstream=_DIL_STREAMS):
    bsz, _, ncls, cls_len, hd = qkv.shape
    seq = ncls * cls_len
    nh = N_HEADS_DIL
    kw = tq + 2 * DIL_HALF
    masks = _dil_mask_tables(tq)

    def spec(off):
        return pl.BlockSpec((None, None, ncls, cls_len, hd), lambda b, h: (b, off + h, 0, 0, 0))

    def two(shape, dtype):
        return [pltpu.VMEM(shape, dtype) for _ in range(2 * nstream)]

    return pl.pallas_call(
        functools.partial(_dil_kernel, tq=tq, nstream=nstream),
        grid=(bsz, nh),
        in_specs=[spec(0), spec(nh), spec(2 * nh),
                  pl.BlockSpec(masks.shape, lambda b, h: (0, 0, 0), pipeline_mode=pl.Buffered(1))],
        out_specs=pl.BlockSpec((None, None, seq, hd), lambda b, h: (b, h, 0, 0)),
        out_shape=jax.ShapeDtypeStruct((bsz, nh, seq, hd), BF16),
        scratch_shapes=([pltpu.VMEM((seq, hd), F32) for _ in range(9)]
                        + two((tq, kw), F32) + two((tq, kw), BF16) + two((kw, hd), BF16)
                        + two((tq, hd), F32) + two((tq, hd), F32)),
        compiler_params=pltpu.CompilerParams(dimension_semantics=("arbitrary", "arbitrary"),
                                             vmem_limit_bytes=DIL_VMEM_LIMIT),
        name="dilated_attention",
    )(qkv, qkv, qkv, masks)


_NA_QROWS = 4
_NA_KROWS = NA_ROWS + _NA_QROWS - 1
_NA_CHUNK = 64


def _na_key_start(i0, rows):
    return np.clip(i0 - NA_ROWS // 2, 0, rows - _NA_KROWS)


def _na_offsets(rows):
    return tuple(sorted({int(i0 - _na_key_start(i0, rows)) for i0 in range(0, rows, _NA_QROWS)}))


def _na_bias_tables(rpb, rows):
    nq = _NA_QROWS * GRID_W
    nk = _NA_KROWS * GRID_W
    nh, nrow, ncol = rpb.shape
    w = GRID_W
    u = jnp.pad(rpb.astype(F32) * LOG2E, ((0, 0), (0, 0), (w - NA_COLS, w + NA_COLS - ncol)))
    toep = jnp.broadcast_to(u[:, :, None, :], (nh, nrow, w, 2 * w)).reshape(nh, nrow, 2 * w * w)
    toep = toep[:, :, :w * (2 * w - 1)].reshape(nh, nrow, w, 2 * w - 1)[:, :, :, w - 1:]
    cs = np.clip(np.arange(w) - NA_COLS // 2, 0, w - NA_COLS)
    cvalid = (np.arange(w)[None, :] >= cs[:, None]) & (np.arange(w)[None, :] < cs[:, None] + NA_COLS)
    toep = jnp.pad(toep, ((0, 0), (_NA_KROWS, _NA_KROWS), (0, 0), (0, 0)))

    def row_pattern(i0):
        ks = int(_na_key_start(i0, rows))
        first, rvalid = [], []
        for qi in range(i0, i0 + _NA_QROWS):
            rs = int(np.clip(qi - NA_ROWS // 2, 0, rows - NA_ROWS))
            first.append(ks - qi + NA_ROWS - 1 + _NA_KROWS)
            rvalid += [rs <= kr < rs + NA_ROWS for kr in range(ks, ks + _NA_KROWS)]
        return first, rvalid

    slabs, rvalids = [], []
    for off in _na_offsets(rows):
        blocks = [i0 for i0 in range(0, rows, _NA_QROWS) if i0 - _na_key_start(i0, rows) == off]
        first, rvalid = row_pattern(blocks[0])
        assert all(row_pattern(i0) == (first, rvalid) for i0 in blocks)
        slabs += [toep[:, lo:lo + _NA_KROWS] for lo in first]
        rvalids += rvalid
    ncase = len(_na_offsets(rows))
    tab = jnp.stack(slabs, axis=1).reshape(nh, ncase, _NA_QROWS, _NA_KROWS, w, w)
    rvalid = np.asarray(rvalids).reshape(ncase, _NA_QROWS, _NA_KROWS)
    valid = rvalid[:, :, :, None, None] & cvalid[None, None, None]
    tab = jnp.where(valid[None], tab, NEG_INF)
    return tab.transpose(0, 1, 3, 5, 2, 4).reshape(nh, ncase, nk, nq)


def _na_kernel(q_ref, k_ref, v_ref, bias_ref, o_ref, s0, s1, p0, p1, l0, l1, *, offsets):
    seq = q_ref.shape[0]
    rows = seq // GRID_W
    nq = _NA_QROWS * GRID_W
    nk = _NA_KROWS * GRID_W
    nblk = rows // _NA_QROWS
    s_buf, p_buf, l_buf = (s0, s1), (p0, p1), (l0, l1)

    def geom(blk):
        blk = jnp.minimum(blk, nblk - 1)
        i0 = blk * _NA_QROWS
        ks = jnp.clip(i0 - NA_ROWS // 2, 0, rows - _NA_KROWS)
        case = sum((i0 - ks >= off).astype(jnp.int32) for off in offsets[1:])
        qrows = pl.ds(pl.multiple_of(i0 * GRID_W, nq), nq)
        krows = pl.ds(pl.multiple_of(ks * GRID_W, GRID_W), nk)
        return qrows, krows, case

    def scores(blk, slot):
        qrows, krows, case = geom(blk)
        st = lax.dot_general(k_ref[krows, :], q_ref[qrows, :], (((1,), (1,)), ((), ())),
                             preferred_element_type=F32)
        s_buf[slot][...] = st + bias_ref[case]

    def softmax(slot):
        s = s_buf[slot]
        nchunk = nk // _NA_CHUNK
        m = s[0:_NA_CHUNK, :]
        for c in range(1, nchunk):
            m = jnp.maximum(m, s[c * _NA_CHUNK:(c + 1) * _NA_CHUNK, :])
        m = jnp.max(m, axis=0, keepdims=True)
        l = jnp.zeros((_NA_CHUNK, nq), F32)
        for c in range(nchunk):
            p = jnp.exp2(s[c * _NA_CHUNK:(c + 1) * _NA_CHUNK, :] - m)
            l = l + p
            p_buf[slot][c * _NA_CHUNK:(c + 1) * _NA_CHUNK, :] = p.astype(BF16)
        l_buf[slot][...] = jnp.broadcast_to(jnp.sum(l, axis=0, keepdims=True), l_buf[slot].shape)

    def values(blk, slot):
        qrows, krows, _ = geom(blk)
        ot = lax.dot_general(v_ref[krows, :], p_buf[slot][...], (((0,), (0,)), ((), ())),
                             preferred_element_type=F32)
        ot = ot * (1.0 / l_buf[slot][0:1, :])
        o_ref[qrows, :] = ot.T.astype(BF16)

    scores(0, 0)
    scores(1, 1)
    softmax(0)

    def body(j, carry):
        w = 2 * j
        scores(w + 2, 0)
        softmax(1)
        values(w, 0)
        scores(w + 3, 1)
        softmax(0)
        values(w + 1, 1)
        return carry

    lax.fori_loop(0, nblk // 2, body, 0)


def _neighborhood_attention(qkv, bias, layer, offsets):
    bsz, _, seq, hd = qkv.shape
    nh = N_HEADS_NA
    ncase, nk, nq = bias.shape[1:]

    def spec(off):
        return pl.BlockSpec((None, None, seq, hd), lambda b, h: (b, off + h, 0, 0))

    return pl.pallas_call(
        functools.partial(_na_kernel, offsets=offsets),
        grid=(bsz, nh),
        in_specs=[spec(0), spec(nh), spec(2 * nh),
                  pl.BlockSpec((None, ncase, nk, nq), lambda b, h: (layer * nh + h, 0, 0, 0))],
        out_specs=pl.BlockSpec((None, None, seq, hd), lambda b, h: (b, h, 0, 0)),
        out_shape=jax.ShapeDtypeStruct((bsz, nh, seq, hd), BF16),
        scratch_shapes=[pltpu.VMEM((nk, nq), F32), pltpu.VMEM((nk, nq), F32),
                        pltpu.VMEM((nk, nq), BF16), pltpu.VMEM((nk, nq), BF16),
                        pltpu.VMEM((16, nq), F32), pltpu.VMEM((16, nq), F32)],
        compiler_params=_cparams(("arbitrary", "arbitrary")),
        name="neighborhood_attention",
    )(qkv, qkv, qkv, bias)


def _out_kernel(a_ref, b_ref, x_ref, mod_ref, w_ref, o_ref, mix_s):
    na = a_ref.shape[0]
    for h in range(na):
        mix_s[:, h * HEAD_DIM:(h + 1) * HEAD_DIM] = a_ref[h]
    for h in range(b_ref.shape[0]):
        mix_s[:, (na + h) * HEAD_DIM:(na + h + 1) * HEAD_DIM] = b_ref[h]
    y = jnp.dot(mix_s[...], w_ref[...], preferred_element_type=F32)
    o_ref[...] = x_ref[...] + mod_ref[2:3, :] * y


def _out_proj(att_a, att_b, x, mod, w, layer, tm=512):
    bsz, seq, d = x.shape
    na, nb = att_a.shape[1], att_b.shape[1]
    return pl.pallas_call(
        _out_kernel,
        grid=(bsz, seq // tm),
        in_specs=[
            pl.BlockSpec((None, na, tm, HEAD_DIM), lambda b, i: (b, 0, i, 0)),
            pl.BlockSpec((None, nb, tm, HEAD_DIM), lambda b, i: (b, 0, i, 0)),
            pl.BlockSpec((None, tm, d), lambda b, i: (b, i, 0)),
            pl.BlockSpec((None, 6, d), lambda b, i: (b, 0, 0)),
            pl.BlockSpec((None, d, d), lambda b, i: (layer, 0, 0), pipeline_mode=pl.Buffered(1)),
        ],
        out_specs=pl.BlockSpec((None, tm, d), lambda b, i: (b, i, 0)),
        out_shape=jax.ShapeDtypeStruct((bsz, seq, d), F32),
        scratch_shapes=[pltpu.VMEM((tm, d), BF16)],
        compiler_params=_cparams(("arbitrary", "arbitrary")),
        name="attn_out_proj",
    )(att_a, att_b, x, mod, w)


def _mlp_in_kernel(x_ref, mod_ref, ln_ref, w_ref, o_ref, h_s):
    @pl.when(pl.program_id(2) == 0)
    def _():
        _norm_modulate(x_ref, ln_ref, mod_ref[4:5, :], mod_ref[3:4, :], h_s)

    y = jnp.maximum(jnp.dot(h_s[...], w_ref[...], preferred_element_type=F32), 0.0)
    o_ref[...] = (y * y).astype(BF16)


def _mlp_in(x, mod, ln, w, layer, tm=1024, tn=2048):
    bsz, seq, d = x.shape
    f = w.shape[2]
    return pl.pallas_call(
        _mlp_in_kernel,
        grid=(bsz, seq // tm, f // tn),
        in_specs=[
            pl.BlockSpec((None, tm, d), lambda b, i, j: (b, i, 0)),
            pl.BlockSpec((None, 6, d), lambda b, i, j: (b, 0, 0)),
            pl.BlockSpec((1, d), lambda b, i, j: (0, 0)),
            pl.BlockSpec((None, d, tn), lambda b, i, j: (layer, 0, j)),
        ],
        out_specs=pl.BlockSpec((None, tm, tn), lambda b, i, j: (b, i, j)),
        out_shape=jax.ShapeDtypeStruct((bsz, seq, f), BF16),
        scratch_shapes=[pltpu.VMEM((tm, d), BF16)],
        compiler_params=_cparams(("arbitrary", "arbitrary", "arbitrary")),
        name="mlp_in",
    )(x, mod, ln, w)


def _mlp_out_kernel(h_ref, x_ref, mod_ref, w_ref, o_ref):
    y = mod_ref[5:6, :] * jnp.dot(h_ref[...], w_ref[...], preferred_element_type=F32)

    @pl.when(pl.program_id(3) == 0)
    def _():
        o_ref[...] = x_ref[...] + y

    @pl.when(pl.program_id(3) != 0)
    def _():
        o_ref[...] += y


def _mlp_out(hid, x, mod, w, layer, tm=1024, tn=1024, tk=4096):
    bsz, seq, d = x.shape
    f = hid.shape[2]
    return pl.pallas_call(
        _mlp_out_kernel,
        grid=(bsz, seq // tm, d // tn, f // tk),
        in_specs=[
            pl.BlockSpec((None, tm, tk), lambda b, i, j, k: (b, i, k)),
            pl.BlockSpec((None, tm, tn), lambda b, i, j, k: (b, i, j)),
            pl.BlockSpec((None, 6, tn), lambda b, i, j, k: (b, 0, j)),
            pl.BlockSpec((None, tk, tn), lambda b, i, j, k: (layer, k, j)),
        ],
        out_specs=pl.BlockSpec((None, tm, tn), lambda b, i, j, k: (b, i, j)),
        out_shape=jax.ShapeDtypeStruct((bsz, seq, d), F32),
        compiler_params=pltpu.CompilerParams(dimension_semantics=("arbitrary",) * 4,
                                             vmem_limit_bytes=BIG_VMEM_LIMIT),
        name="mlp_out",
    )(hid, x, mod, w)


def _forward(x, c, ln1, w_ada, b_ada, w_in, q_norm_dil, k_norm_dil, q_norm_na, k_norm_na,
             na_rel_bias, w_out, ln2, w_mlp_in, w_mlp_out, dil_tq):
    bsz, seq, d = x.shape
    depth = w_ada.shape[0]
    rows = seq // GRID_W
    tables = _rope_tables(seq, _QKV_TM)
    mod_all = _modulation(c, w_ada, b_ada).reshape(depth, bsz, 6, d)
    w_in, w_out, w_mlp_in, w_mlp_out = (w.astype(BF16) for w in (w_in, w_out, w_mlp_in, w_mlp_out))
    na_bias = _na_bias_tables(na_rel_bias.reshape((-1,) + na_rel_bias.shape[2:]), rows)
    for layer in range(depth):
        mod = mod_all[layer]
        gains = jnp.stack([q_norm_dil[layer], k_norm_dil[layer],
                           q_norm_na[layer], k_norm_na[layer]]).astype(F32)
        qkv_dil, qkv_na = _qkv_proj(x, mod, ln1[layer][None, :], w_in, layer, gains, tables)
        att_dil = _dilated_attention(qkv_dil, tq=dil_tq)
        att_na = _neighborhood_attention(qkv_na, na_bias, layer, _na_offsets(rows))
        x = _out_proj(att_dil, att_na, x, mod, w_out, layer)
        hid = _mlp_in(x, mod, ln2[layer][None, :], w_mlp_in, layer)
        x = _mlp_out(hid, x, mod, w_mlp_out, layer)
    return x


def kernel(x, c, ln1, w_ada, b_ada, w_in, q_norm_dil, k_norm_dil, q_norm_na, k_norm_na,
           na_rel_bias, w_out, ln2, w_mlp_in, w_mlp_out):
    return _forward(x, c, ln1, w_ada, b_ada, w_in, q_norm_dil, k_norm_dil, q_norm_na,
                    k_norm_na, na_rel_bias, w_out, ln2, w_mlp_in, w_mlp_out, dil_tq=128)
```

```python
import functools

import numpy as np
import jax
import jax.numpy as jnp
from jax import lax
from jax.experimental import pallas as pl
from jax.experimental.pallas import tpu as pltpu

HEAD_DIM = 128
N_HEADS_DIL = 8
N_HEADS_NA = 8
ROT_DIM = HEAD_DIM // 4
ROPE_THETA = 500000.0
DIL_HALF = 64
DILATIONS = (16, 4, 1)
GRID_W = 64
NA_ROWS = 8
NA_COLS = 16
EPS = 1e-6
NEG_INF = -1e30
LOG2E = float(np.log2(np.e))

V7X_VMEM_BYTES = 64 * 1024 * 1024
VMEM_LIMIT = 56 * 1024 * 1024
BIG_VMEM_LIMIT = 60 * 1024 * 1024
DIL_VMEM_LIMIT = BIG_VMEM_LIMIT

F32 = jnp.float32
BF16 = jnp.bfloat16


def _cparams(sem):
    return pltpu.CompilerParams(dimension_semantics=sem, vmem_limit_bytes=VMEM_LIMIT)


def _mod_kernel(ct_ref, w_ref, b_ref, o_ref, act_s, *, nb):
    d = w_ref.shape[0]
    tn = w_ref.shape[1]

    @pl.when((pl.program_id(0) == 0) & (pl.program_id(1) == 0))
    def _():
        ct = ct_ref[...]
        act = ct * (1.0 / (1.0 + jnp.exp(-ct)))
        for b in range(nb):
            act_s[b] = jnp.broadcast_to(act[:, b:b + 1], (d, HEAD_DIM))

    def body(k, accs):
        r0 = pl.multiple_of(k * 8, 8)
        w8 = w_ref[pl.ds(r0, 8), :]
        out = []
        for b, acc in enumerate(accs):
            a = act_s[b, pl.ds(r0, 8), :]
            out.append(acc + w8 * jnp.concatenate([a] * (tn // HEAD_DIM), axis=1))
        return tuple(out)

    accs = lax.fori_loop(0, d // 8, body, tuple(jnp.zeros((8, tn), F32) for _ in range(nb)),
                         unroll=4)
    for b in range(nb):
        o_ref[b:b + 1, :] = jnp.sum(accs[b], axis=0, keepdims=True) + b_ref[...]


def _modulation(c, w_ada, b_ada):
    nl, d, n6 = w_ada.shape
    nb = c.shape[0]
    tn = 1024
    return pl.pallas_call(
        functools.partial(_mod_kernel, nb=nb),
        grid=(nl, n6 // tn),
        in_specs=[
            pl.BlockSpec((d, nb), lambda l, j: (0, 0)),
            pl.BlockSpec((None, d, tn), lambda l, j: (l, 0, j)),
            pl.BlockSpec((None, 1, tn), lambda l, j: (l, 0, j)),
        ],
        out_specs=pl.BlockSpec((None, nb, tn), lambda l, j: (l, 0, j)),
        out_shape=jax.ShapeDtypeStruct((nl, nb, n6), F32),
        scratch_shapes=[pltpu.VMEM((nb, d, HEAD_DIM), F32)],
        compiler_params=_cparams(("arbitrary", "arbitrary")),
        name="adaln_modulation",
    )(c.T, w_ada, b_ada.reshape(nl, 1, n6))


_NORM_CHUNK = 64


def _norm_modulate(x_ref, ln_ref, sc, sh, h_s, straight_line=False):
    tm = x_ref.shape[0]
    gain = ln_ref[...] * (1.0 + sc)

    def body(c, carry):
        r0 = c * _NORM_CHUNK if straight_line else pl.multiple_of(c * _NORM_CHUNK, _NORM_CHUNK)
        x = x_ref[pl.ds(r0, _NORM_CHUNK), :]
        ms = jnp.mean(x * x, axis=-1, keepdims=True)
        h_s[pl.ds(r0, _NORM_CHUNK), :] = (x * lax.rsqrt(ms + EPS) * gain + sh).astype(BF16)
        return carry

    if straight_line:
        for c in range(tm // _NORM_CHUNK):
            body(c, 0)
    else:
        lax.fori_loop(0, tm // _NORM_CHUNK, body, 0)


def _qkv_kernel(x_ref, xnext_ref, mod_ref, ln_ref, perm_ref, w_ref, gain_ref, cos_ref, sa_ref,
                sb_ref, oc_ref, on_ref, ha_s, hb_s, hc_s, acc_s):
    t = pl.program_id(1)

    @pl.when(t == 0)
    def _():
        _norm_modulate(x_ref, ln_ref, mod_ref[1:2, :], mod_ref[0:1, :], ha_s)

    @pl.when(t % 2 == 0)
    def _():
        _qkv_tile(xnext_ref, mod_ref, ln_ref, perm_ref, w_ref, gain_ref, cos_ref, sa_ref, sb_ref,
                  oc_ref, on_ref, ha_s, hb_s, hc_s, acc_s)

    @pl.when(t % 2 == 1)
    def _():
        _qkv_tile(xnext_ref, mod_ref, ln_ref, perm_ref, w_ref, gain_ref, cos_ref, sa_ref, sb_ref,
                  oc_ref, on_ref, hb_s, ha_s, hc_s, acc_s)


def _qkv_tile(xnext_ref, mod_ref, ln_ref, perm_ref, w_ref, gain_ref, cos_ref, sa_ref, sb_ref,
              oc_ref, on_ref, hn_s, hnext_s, hc_s, acc_s):
    scale = HEAD_DIM ** -0.5 * LOG2E
    nh = N_HEADS_DIL
    tn = nh * HEAD_DIM
    tm = hn_s.shape[0]
    ncls = oc_ref.shape[1]
    _norm_modulate(xnext_ref, ln_ref, mod_ref[1:2, :], mod_ref[0:1, :], hnext_s, straight_line=True)
    hc_s[...] = jnp.dot(perm_ref[...], hn_s[...], preferred_element_type=F32).astype(BF16)

    for g in range(w_ref.shape[1] // tn):
        rope = g < 3
        h_s = hc_s if rope else hn_s
        acc = acc_s.at[g % 2]
        acc[...] = jnp.dot(h_s[...], w_ref[:, g * tn:(g + 1) * tn], preferred_element_type=F32)
        kind = g % 3
        if kind < 2:
            row = 2 * (g // 3) + kind
            gain = gain_ref[row:row + 1, :] * (scale if kind == 0 else 1.0)
        for h in range(nh):
            y = acc[:, h * HEAD_DIM:(h + 1) * HEAD_DIM]
            if kind < 2:
                ms = jnp.mean(y * y, axis=-1, keepdims=True)
                y = y * lax.rsqrt(ms + EPS) * gain
            if rope and kind < 2:
                y = (y * cos_ref[...]
                     + pltpu.roll(y, ROT_DIM // 2, 1) * sa_ref[...]
                     + pltpu.roll(y, HEAD_DIM - ROT_DIM // 2, 1) * sb_ref[...])
            if rope:
                oc_ref[kind * nh + h] = y.astype(BF16).reshape(ncls, tm // ncls, HEAD_DIM)
            else:
                on_ref[kind * nh + h] = y.astype(BF16)


def _class_major_positions(seq, tm):
    dmax = DILATIONS[0]
    pos = np.arange(seq).reshape(seq // tm, tm // dmax, dmax)
    return pos.transpose(0, 2, 1).reshape(seq)


def _rope_tables(seq, tm):
    half = ROT_DIM // 2
    pos = _class_major_positions(seq, tm).astype(np.float64)[:, None]
    inv = ROPE_THETA ** (-np.arange(0, ROT_DIM, 2, dtype=np.float64) / ROT_DIM)[None, :]
    ang = pos * inv
    cos = np.ones((seq, HEAD_DIM), np.float64)
    sa = np.zeros((seq, HEAD_DIM), np.float64)
    sb = np.zeros((seq, HEAD_DIM), np.float64)
    cos[:, :half] = np.cos(ang)
    cos[:, half:ROT_DIM] = np.cos(ang)
    sa[:, half:ROT_DIM] = np.sin(ang)
    sb[:, :half] = -np.sin(ang)
    return (jnp.asarray(cos, F32), jnp.asarray(sa, F32), jnp.asarray(sb, F32))


_QKV_TM = 256


def _qkv_proj(x, mod, ln, w, layer, gains, tables, tm=_QKV_TM):
    bsz, seq, d = x.shape
    n = w.shape[2]
    tn = N_HEADS_DIL * HEAD_DIM
    nslab = n // HEAD_DIM // 2
    dmax = DILATIONS[0]
    cos, sa, sb = tables
    perm = _class_major_positions(tm, tm)
    perm_mat = jnp.asarray(np.eye(tm)[perm], BF16)
    tab_spec = pl.BlockSpec((tm, HEAD_DIM), lambda b, t: (t, 0))
    qkv_dil, qkv_na = pl.pallas_call(
        _qkv_kernel,
        grid=(bsz, seq // tm),
        in_specs=[
            pl.BlockSpec((None, tm, d), lambda b, t: (b, t, 0)),
            pl.BlockSpec((None, tm, d), lambda b, t: (b, jnp.minimum(t + 1, seq // tm - 1), 0)),
            pl.BlockSpec((None, 6, d), lambda b, t: (b, 0, 0)),
            pl.BlockSpec((1, d), lambda b, t: (0, 0)),
            pl.BlockSpec((tm, tm), lambda b, t: (0, 0)),
            pl.BlockSpec((None, d, n), lambda b, t: (layer, 0, 0), pipeline_mode=pl.Buffered(1)),
            pl.BlockSpec((4, HEAD_DIM), lambda b, t: (0, 0)),
            tab_spec, tab_spec, tab_spec,
        ],
        out_specs=[
            pl.BlockSpec((None, nslab, dmax, tm // dmax, HEAD_DIM), lambda b, t: (b, 0, 0, t, 0)),
            pl.BlockSpec((None, nslab, tm, HEAD_DIM), lambda b, t: (b, 0, t, 0)),
        ],
        out_shape=[jax.ShapeDtypeStruct((bsz, nslab, dmax, seq // dmax, HEAD_DIM), BF16),
                   jax.ShapeDtypeStruct((bsz, nslab, seq, HEAD_DIM), BF16)],
        scratch_shapes=[pltpu.VMEM((tm, d), BF16), pltpu.VMEM((tm, d), BF16),
                        pltpu.VMEM((tm, d), BF16), pltpu.VMEM((2, tm, tn), F32)],
        compiler_params=_cparams(("arbitrary", "arbitrary")),
        name="qkv_proj",
    )(x, x, mod, ln, perm_mat, w, gains, cos, sa, sb)
    return qkv_dil, qkv_na


_CAST_CHUNK = 256
_DIL_STREAMS = 4


def _dil_mask_tables(tq):
    kw = tq + 2 * DIL_HALF
    i = np.arange(tq)[:, None]
    j = np.arange(kw)[None, :]
    tabs = [np.where(np.abs(j - i - DIL_HALF * case) <= DIL_HALF, 0.0, NEG_INF) for case in range(3)]
    return jnp.asarray(np.stack(tabs), F32)


def _dil_kernel(q_ref, k_ref, v_ref, mask_ref, o_ref, a0, a1, a2, b0, b1, b2, c0, c1, c2,
                *stage_bufs, tq, nstream):
    seq = o_ref.shape[0]
    kw = tq + 2 * DIL_HALF
    nbuf = 2 * nstream
    s_buf, p_buf, v_buf, m_buf, l_buf = (stage_bufs[i * nbuf:(i + 1) * nbuf] for i in range(5))
    set_a, set_b, set_c = (a0, a1, a2), (b0, b1, b2), (c0, c1, c2)
    step = DILATIONS[0] // DILATIONS[1]

    def finer_rows(dil, r, n0, size):
        fine = dil // step
        return pl.ds((r % fine) * (seq // fine) + r // fine + step * n0, size, stride=step)

    def class_rows(ref, cls_len, r, start, size, align):
        if len(ref.shape) == 3:
            return ref[r, pl.ds(pl.multiple_of(start, align), size), :]
        return ref[pl.ds(pl.multiple_of(r * cls_len + start, align), size), :]

    def reorder_chunk(dil, src, dst, it):
        cls_len = seq // dil
        nchunk = cls_len // _CAST_CHUNK
        r = it // nchunk
        n0 = (it % nchunk) * _CAST_CHUNK
        for s_ref, d_ref in zip(src, dst):
            chunk = class_rows(s_ref, cls_len, r, n0, _CAST_CHUNK, _CAST_CHUNK)
            d_ref[finer_rows(dil, r, n0, _CAST_CHUNK), :] = chunk.astype(F32)

    def reorder(dil, src, dst):
        def body(it, carry):
            reorder_chunk(dil, src, dst, it)
            return carry

        lax.fori_loop(0, seq // _CAST_CHUNK, body, 0)

    def run_pattern(dil, mode, src, acc_in, acc_out, side_work=None, side_chunks=0):
        cls_len = seq // dil
        nblk = cls_len // tq
        total = dil * nblk
        per_stream = total // nstream
        qs, ks, vs = src

        def geom(g, it):
            it = g * per_stream + jnp.minimum(it, per_stream - 1)
            r = it // nblk
            n0 = (it % nblk) * tq
            nk0 = jnp.clip(n0 - DIL_HALF, 0, cls_len - kw)
            qrows = pl.ds(pl.multiple_of(r * cls_len + n0, tq), tq)
            return qrows, nk0, (n0 - nk0) // DIL_HALF, r, n0

        def scores(g, it, slot):
            slot = 2 * g + slot
            _, nk0, case, r, n0 = geom(g, it)
            q = class_rows(qs, cls_len, r, n0, tq, tq).astype(BF16)
            k = class_rows(ks, cls_len, r, nk0, kw, DIL_HALF).astype(BF16)
            v_buf[slot][...] = class_rows(vs, cls_len, r, nk0, kw, DIL_HALF).astype(BF16)
            s = lax.dot_general(q, k, (((1,), (1,)), ((), ())), preferred_element_type=F32)
            s_buf[slot][...] = s + mask_ref[case]

        def softmax(g, slot):
            slot = 2 * g + slot
            s = s_buf[slot][...]
            m = jnp.max(s, axis=-1, keepdims=True)
            p = jnp.exp2(s - m)
            l = jnp.sum(p, axis=-1, keepdims=True)
            p_buf[slot][...] = p.astype(BF16)
            m_buf[slot][...] = jnp.broadcast_to(m, (tq, HEAD_DIM))
            l_buf[slot][...] = jnp.broadcast_to(l, (tq, HEAD_DIM))

        def values(g, it, slot):
            slot = 2 * g + slot
            qrows, _, _, r, n0 = geom(g, it)
            o = jnp.dot(p_buf[slot][...], v_buf[slot][...], preferred_element_type=F32)
            mb = m_buf[slot][...]
            lb = l_buf[slot][...]
            if mode != "init":
                acc_i, m_i, l_i = acc_in
                m_old = m_i[qrows, :]
                m_new = jnp.maximum(m_old, mb)
                a_old = jnp.exp2(m_old - m_new)
                a_new = jnp.exp2(mb - m_new)
                o = acc_i[qrows, :] * a_old + o * a_new
                lb = l_i[qrows, :] * a_old + lb * a_new
                mb = m_new
            if mode == "final":
                o_ref[qrows, :] = (o / lb).astype(BF16)
            else:
                acc_o, m_o, l_o = acc_out
                orows = finer_rows(dil, r, n0, tq)
                acc_o[orows, :] = o
                m_o[orows, :] = mb
                l_o[orows, :] = lb

        for g in range(nstream):
            scores(g, 0, 0)
            scores(g, 1, 1)
            softmax(g, 0)

        def body(j, carry):
            w = 2 * j
            for g in range(nstream):
                values(g, w, 0)
                scores(g, w + 2, 0)
                softmax(g, 1)
                values(g, w + 1, 1)
                scores(g, w + 3, 1)
                softmax(g, 0)
                if side_work is not None:
                    for c in range(side_chunks):
                        side_work((g * per_stream + w) * side_chunks // 2 + c)
            return carry

        lax.fori_loop(0, per_stream // 2, body, 0)

    d16, d4, d1 = DILATIONS
    inputs = (q_ref, k_ref, v_ref)
    nchunks, nblocks = seq // _CAST_CHUNK, seq // tq
    if (2 * nchunks) % nblocks == 0 and (nblocks // 2) % nstream == 0:
        run_pattern(d16, "init", inputs, None, set_a, side_chunks=2 * nchunks // nblocks,
                    side_work=lambda i: reorder_chunk(d16, inputs, set_b, i))
    else:
        run_pattern(d16, "init", inputs, None, set_a)
        reorder(d16, inputs, set_b)
    run_pattern(d4, "merge", set_b, set_a, set_c)
    reorder(d4, set_b, set_a)
    run_pattern(d1, "final", set_a, set_c, None)


def _dilated_attention(qkv, tq=256, nstream=_DIL_STREAMS):
    bsz, _, ncls, cls_len, hd = qkv.shape
    seq = ncls * cls_len
    nh = N_HEADS_DIL
    kw = tq + 2 * DIL_HALF
    masks = _dil_mask_tables(tq)

    def spec(off):
        return pl.BlockSpec((None, None, ncls, cls_len, hd), lambda b, h: (b, off + h, 0, 0, 0))

    def two(shape, dtype):
        return [pltpu.VMEM(shape, dtype) for _ in range(2 * nstream)]

    return pl.pallas_call(
        functools.partial(_dil_kernel, tq=tq, nstream=nstream),
        grid=(bsz, nh),
        in_specs=[spec(0), spec(nh), spec(2 * nh),
                  pl.BlockSpec(masks.shape, lambda b, h: (0, 0, 0), pipeline_mode=pl.Buffered(1))],
        out_specs=pl.BlockSpec((None, None, seq, hd), lambda b, h: (b, h, 0, 0)),
        out_shape=jax.ShapeDtypeStruct((bsz, nh, seq, hd), BF16),
        scratch_shapes=([pltpu.VMEM((seq, hd), F32) for _ in range(9)]
                        + two((tq, kw), F32) + two((tq, kw), BF16) + two((kw, hd), BF16)
                        + two((tq, hd), F32) + two((tq, hd), F32)),
        compiler_params=pltpu.CompilerParams(dimension_semantics=("arbitrary", "arbitrary"),
                                             vmem_limit_bytes=DIL_VMEM_LIMIT),
        name="dilated_attention",
    )(qkv, qkv, qkv, masks)


_NA_QROWS = 4
_NA_KROWS = NA_ROWS + _NA_QROWS - 1
_NA_CHUNK = 64


def _na_key_start(i0, rows):
    return np.clip(i0 - NA_ROWS // 2, 0, rows - _NA_KROWS)


def _na_offsets(rows):
    return tuple(sorted({int(i0 - _na_key_start(i0, rows)) for i0 in range(0, rows, _NA_QROWS)}))


def _na_bias_tables(rpb, rows):
    nq = _NA_QROWS * GRID_W
    nk = _NA_KROWS * GRID_W
    nh, nrow, ncol = rpb.shape
    w = GRID_W
    u = jnp.pad(rpb.astype(F32) * LOG2E, ((0, 0), (0, 0), (w - NA_COLS, w + NA_COLS - ncol)))
    toep = jnp.broadcast_to(u[:, :, None, :], (nh, nrow, w, 2 * w)).reshape(nh, nrow, 2 * w * w)
    toep = toep[:, :, :w * (2 * w - 1)].reshape(nh, nrow, w, 2 * w - 1)[:, :, :, w - 1:]
    cs = np.clip(np.arange(w) - NA_COLS // 2, 0, w - NA_COLS)
    cvalid = (np.arange(w)[None, :] >= cs[:, None]) & (np.arange(w)[None, :] < cs[:, None] + NA_COLS)
    toep = jnp.pad(toep, ((0, 0), (_NA_KROWS, _NA_KROWS), (0, 0), (0, 0)))

    def row_pattern(i0):
        ks = int(_na_key_start(i0, rows))
        first, rvalid = [], []
        for qi in range(i0, i0 + _NA_QROWS):
            rs = int(np.clip(qi - NA_ROWS // 2, 0, rows - NA_ROWS))
            first.append(ks - qi + NA_ROWS - 1 + _NA_KROWS)
            rvalid += [rs <= kr < rs + NA_ROWS for kr in range(ks, ks + _NA_KROWS)]
        return first, rvalid

    slabs, rvalids = [], []
    for off in _na_offsets(rows):
        blocks = [i0 for i0 in range(0, rows, _NA_QROWS) if i0 - _na_key_start(i0, rows) == off]
        first, rvalid = row_pattern(blocks[0])
        assert all(row_pattern(i0) == (first, rvalid) for i0 in blocks)
        slabs += [toep[:, lo:lo + _NA_KROWS] for lo in first]
        rvalids += rvalid
    ncase = len(_na_offsets(rows))
    tab = jnp.stack(slabs, axis=1).reshape(nh, ncase, _NA_QROWS, _NA_KROWS, w, w)
    rvalid = np.asarray(rvalids).reshape(ncase, _NA_QROWS, _NA_KROWS)
    valid = rvalid[:, :, :, None, None] & cvalid[None, None, None]
    tab = jnp.where(valid[None], tab, NEG_INF)
    return tab.transpose(0, 1, 3, 5, 2, 4).reshape(nh, ncase, nk, nq)


def _na_kernel(q_ref, k_ref, v_ref, bias_ref, o_ref, *stage_bufs, offsets, nstream):
    seq = q_ref.shape[0]
    rows = seq // GRID_W
    nq = _NA_QROWS * GRID_W
    nk = _NA_KROWS * GRID_W
    nblk = rows // _NA_QROWS
    per_stream = nblk // nstream
    nbuf = 2 * nstream
    s_buf, p_buf, l_buf = (stage_bufs[i * nbuf:(i + 1) * nbuf] for i in range(3))

    def geom(g, blk):
        blk = g * per_stream + jnp.minimum(blk, per_stream - 1)
        i0 = blk * _NA_QROWS
        ks = jnp.clip(i0 - NA_ROWS // 2, 0, rows - _NA_KROWS)
        case = sum((i0 - ks >= off).astype(jnp.int32) for off in offsets[1:])
        qrows = pl.ds(pl.multiple_of(i0 * GRID_W, nq), nq)
        krows = pl.ds(pl.multiple_of(ks * GRID_W, GRID_W), nk)
        return qrows, krows, case

    def scores(g, blk, slot):
        slot = 2 * g + slot
        qrows, krows, case = geom(g, blk)
        st = lax.dot_general(k_ref[krows, :], q_ref[qrows, :], (((1,), (1,)), ((), ())),
                             preferred_element_type=F32)
        s_buf[slot][...] = st + bias_ref[case]

    def softmax(g, slot):
        slot = 2 * g + slot
        s = s_buf[slot]
        nchunk = nk // _NA_CHUNK
        m = s[0:_NA_CHUNK, :]
        for c in range(1, nchunk):
            m = jnp.maximum(m, s[c * _NA_CHUNK:(c + 1) * _NA_CHUNK, :])
        m = jnp.max(m, axis=0, keepdims=True)
        l = jnp.zeros((_NA_CHUNK, nq), F32)
        for c in range(nchunk):
            p = jnp.exp2(s[c * _NA_CHUNK:(c + 1) * _NA_CHUNK, :] - m)
            l = l + p
            p_buf[slot][c * _NA_CHUNK:(c + 1) * _NA_CHUNK, :] = p.astype(BF16)
        l_buf[slot][...] = jnp.broadcast_to(jnp.sum(l, axis=0, keepdims=True), l_buf[slot].shape)

    def values(g, blk, slot):
        slot = 2 * g + slot
        qrows, krows, _ = geom(g, blk)
        ot = lax.dot_general(v_ref[krows, :], p_buf[slot][...], (((0,), (0,)), ((), ())),
                             preferred_element_type=F32)
        ot = ot * (1.0 / l_buf[slot][0:1, :])
        o_ref[qrows, :] = ot.T.astype(BF16)

    for g in range(nstream):
        scores(g, 0, 0)
        scores(g, 1, 1)
        softmax(g, 0)

    def body(j, carry):
        w = 2 * j
        for g in range(nstream):
            scores(g, w + 2, 0)
            softmax(g, 1)
            values(g, w, 0)
            scores(g, w + 3, 1)
            softmax(g, 0)
            values(g, w + 1, 1)
        return carry

    lax.fori_loop(0, per_stream // 2, body, 0)


_NA_STREAMS = 2


def _neighborhood_attention(qkv, bias, layer, offsets, nstream=_NA_STREAMS):
    bsz, _, seq, hd = qkv.shape
    nh = N_HEADS_NA
    ncase, nk, nq = bias.shape[1:]

    def spec(off):
        return pl.BlockSpec((None, None, seq, hd), lambda b, h: (b, off + h, 0, 0))

    def slots(shape, dtype):
        return [pltpu.VMEM(shape, dtype) for _ in range(2 * nstream)]

    return pl.pallas_call(
        functools.partial(_na_kernel, offsets=offsets, nstream=nstream),
        grid=(bsz, nh),
        in_specs=[spec(0), spec(nh), spec(2 * nh),
                  pl.BlockSpec((None, ncase, nk, nq), lambda b, h: (layer * nh + h, 0, 0, 0))],
        out_specs=pl.BlockSpec((None, None, seq, hd), lambda b, h: (b, h, 0, 0)),
        out_shape=jax.ShapeDtypeStruct((bsz, nh, seq, hd), BF16),
        scratch_shapes=(slots((nk, nq), F32) + slots((nk, nq), BF16) + slots((16, nq), F32)),
        compiler_params=_cparams(("arbitrary", "arbitrary")),
        name="neighborhood_attention",
    )(qkv, qkv, qkv, bias)


def _out_kernel(a_ref, b_ref, x_ref, mod_ref, w_ref, o_ref, mix_s):
    na = a_ref.shape[0]
    for h in range(na):
        mix_s[:, h * HEAD_DIM:(h + 1) * HEAD_DIM] = a_ref[h]
    for h in range(b_ref.shape[0]):
        mix_s[:, (na + h) * HEAD_DIM:(na + h + 1) * HEAD_DIM] = b_ref[h]
    y = jnp.dot(mix_s[...], w_ref[...], preferred_element_type=F32)
    o_ref[...] = x_ref[...] + mod_ref[2:3, :] * y


def _out_proj(att_a, att_b, x, mod, w, layer, tm=512):
    bsz, seq, d = x.shape
    na, nb = att_a.shape[1], att_b.shape[1]
    return pl.pallas_call(
        _out_kernel,
        grid=(bsz, seq // tm),
        in_specs=[
            pl.BlockSpec((None, na, tm, HEAD_DIM), lambda b, i: (b, 0, i, 0)),
            pl.BlockSpec((None, nb, tm, HEAD_DIM), lambda b, i: (b, 0, i, 0)),
            pl.BlockSpec((None, tm, d), lambda b, i: (b, i, 0)),
            pl.BlockSpec((None, 6, d), lambda b, i: (b, 0, 0)),
            pl.BlockSpec((None, d, d), lambda b, i: (layer, 0, 0), pipeline_mode=pl.Buffered(1)),
        ],
        out_specs=pl.BlockSpec((None, tm, d), lambda b, i: (b, i, 0)),
        out_shape=jax.ShapeDtypeStruct((bsz, seq, d), F32),
        scratch_shapes=[pltpu.VMEM((tm, d), BF16)],
        compiler_params=_cparams(("arbitrary", "arbitrary")),
        name="attn_out_proj",
    )(att_a, att_b, x, mod, w)


def _mlp_in_kernel(x_ref, mod_ref, ln_ref, w_ref, o_ref, h_s):
    @pl.when(pl.program_id(2) == 0)
    def _():
        _norm_modulate(x_ref, ln_ref, mod_ref[4:5, :], mod_ref[3:4, :], h_s)

    y = jnp.maximum(jnp.dot(h_s[...], w_ref[...], preferred_element_type=F32), 0.0)
    o_ref[...] = (y * y).astype(BF16)


def _mlp_in(x, mod, ln, w, layer, tm=1024, tn=2048):
    bsz, seq, d = x.shape
    f = w.shape[2]
    return pl.pallas_call(
        _mlp_in_kernel,
        grid=(bsz, seq // tm, f // tn),
        in_specs=[
            pl.BlockSpec((None, tm, d), lambda b, i, j: (b, i, 0)),
            pl.BlockSpec((None, 6, d), lambda b, i, j: (b, 0, 0)),
            pl.BlockSpec((1, d), lambda b, i, j: (0, 0)),
            pl.BlockSpec((None, d, tn), lambda b, i, j: (layer, 0, j)),
        ],
        out_specs=pl.BlockSpec((None, tm, tn), lambda b, i, j: (b, i, j)),
        out_shape=jax.ShapeDtypeStruct((bsz, seq, f), BF16),
        scratch_shapes=[pltpu.VMEM((tm, d), BF16)],
        compiler_params=_cparams(("arbitrary", "arbitrary", "arbitrary")),
        name="mlp_in",
    )(x, mod, ln, w)


def _mlp_out_kernel(h_ref, x_ref, mod_ref, w_ref, o_ref):
    y = mod_ref[5:6, :] * jnp.dot(h_ref[...], w_ref[...], preferred_element_type=F32)

    @pl.when(pl.program_id(3) == 0)
    def _():
        o_ref[...] = x_ref[...] + y

    @pl.when(pl.program_id(3) != 0)
    def _():
        o_ref[...] += y


def _mlp_out(hid, x, mod, w, layer, tm=1024, tn=1024, tk=4096):
    bsz, seq, d = x.shape
    f = hid.shape[2]
    return pl.pallas_call(
        _mlp_out_kernel,
        grid=(bsz, seq // tm, d // tn, f // tk),
        in_specs=[
            pl.BlockSpec((None, tm, tk), lambda b, i, j, k: (b, i, k)),
            pl.BlockSpec((None, tm, tn), lambda b, i, j, k: (b, i, j)),
            pl.BlockSpec((None, 6, tn), lambda b, i, j, k: (b, 0, j)),
            pl.BlockSpec((None, tk, tn), lambda b, i, j, k: (layer, k, j)),
        ],
        out_specs=pl.BlockSpec((None, tm, tn), lambda b, i, j, k: (b, i, j)),
        out_shape=jax.ShapeDtypeStruct((bsz, seq, d), F32),
        compiler_params=pltpu.CompilerParams(dimension_semantics=("arbitrary",) * 4,
                                             vmem_limit_bytes=BIG_VMEM_LIMIT),
        name="mlp_out",
    )(hid, x, mod, w)


def _forward(x, c, ln1, w_ada, b_ada, w_in, q_norm_dil, k_norm_dil, q_norm_na, k_norm_na,
             na_rel_bias, w_out, ln2, w_mlp_in, w_mlp_out, dil_tq):
    bsz, seq, d = x.shape
    depth = w_ada.shape[0]
    rows = seq // GRID_W
    tables = _rope_tables(seq, _QKV_TM)
    mod_all = _modulation(c, w_ada, b_ada).reshape(depth, bsz, 6, d)
    w_in, w_out, w_mlp_in, w_mlp_out = (w.astype(BF16) for w in (w_in, w_out, w_mlp_in, w_mlp_out))
    na_bias = _na_bias_tables(na_rel_bias.reshape((-1,) + na_rel_bias.shape[2:]), rows)
    for layer in range(depth):
        mod = mod_all[layer]
        gains = jnp.stack([q_norm_dil[layer], k_norm_dil[layer],
                           q_norm_na[layer], k_norm_na[layer]]).astype(F32)
        qkv_dil, qkv_na = _qkv_proj(x, mod, ln1[layer][None, :], w_in, layer, gains, tables)
        att_dil = _dilated_attention(qkv_dil, tq=dil_tq)
        att_na = _neighborhood_attention(qkv_na, na_bias, layer, _na_offsets(rows))
        x = _out_proj(att_dil, att_na, x, mod, w_out, layer)
        hid = _mlp_in(x, mod, ln2[layer][None, :], w_mlp_in, layer)
        x = _mlp_out(hid, x, mod, w_mlp_out, layer)
    return x


def kernel(x, c, ln1, w_ada, b_ada, w_in, q_norm_dil, k_norm_dil, q_norm_na, k_norm_na,
           na_rel_bias, w_out, ln2, w_mlp_in, w_mlp_out):
    return _forward(x, c, ln1, w_ada, b_ada, w_in, q_norm_dil, k_norm_dil, q_norm_na,
                    k_norm_na, na_rel_bias, w_out, ln2, w_mlp_in, w_mlp_out, dil_tq=128)
```
